```python
import jax
import jax.numpy as jnp
from jax import lax
import numpy as np


D_MODEL = 1024
BATCH = 8
SEQ = 4096
DEPTH = 2

GRID_W = 64
ROPE_THETA = 10000.0
NORM_EPS = 1e-6
NEG_INF = -1e30

MLA_HEADS = 4
MLA_NOPE = 128
MLA_ROPE = 64
MLA_V = 128
MLA_Q_RANK = 256
MLA_KV_RANK = 128
MLA_QBLOCK = 128
DIL_PATTERNS = ((128, 1), (512, 4), (2048, 16))
DIL_HEADS = 4
DIL_DH = 64
DIL_QBLOCK = 64
NAT_HEADS = 8
NAT_DH = 64
NAT_KH_MAX = 8
NAT_KW = 16
NAT_QCOL = 16
NAT_KCOL = NAT_QCOL + NAT_KW
GLA_HEADS = 4
GLA_DK = 64
GLA_DV = 128
GLA_GATE_RANK = 16
GLA_TAU = 16.0
GLA_CHUNK = 64

W_A = MLA_HEADS * MLA_V
W_B = DIL_HEADS * DIL_DH
W_C = NAT_HEADS * NAT_DH
W_D = GLA_HEADS * GLA_DV
N_BRANCH = 4

IN_SIZES = (MLA_Q_RANK, MLA_KV_RANK, MLA_ROPE,
            3 * len(DIL_PATTERNS) * W_B,
            W_C, W_C, W_C,
            GLA_HEADS * GLA_DK, GLA_HEADS * GLA_DK, W_D, GLA_GATE_RANK, GLA_GATE_RANK,
            W_A, W_B, W_C, W_D)
IN_SPLITS = tuple(int(s) for s in np.cumsum(IN_SIZES)[:-1])
D_IN = int(sum(IN_SIZES))

kernel_name = 'hybrid_gated_mla_dilated_nat_gla_encoder'


def rms_norm(x, g):
    xf = x.astype(jnp.float32)
    y = xf * lax.rsqrt(jnp.mean(xf * xf, axis=-1, keepdims=True) + NORM_EPS)
    return (y * g.astype(jnp.float32)).astype(x.dtype)


def rope(x, pos):
    dim = x.shape[-1]
    inv = jnp.power(ROPE_THETA, -jnp.arange(0, dim, 2, dtype=jnp.float32) / dim)
    ang = pos.astype(jnp.float32)[:, None] * inv[None, :]
    cos = jnp.cos(ang)[None, :, None, :]
    sin = jnp.sin(ang)[None, :, None, :]
    xf = x.astype(jnp.float32)
    x1, x2 = xf[..., :dim // 2], xf[..., dim // 2:]
    return jnp.concatenate([x1 * cos - x2 * sin, x2 * cos + x1 * sin], axis=-1).astype(x.dtype)


def mla_attention(q_lat, kv_lat, k_rope, q_norm_g, w_uq, kv_norm_g, w_ukv, pos):
    B_, S_, _ = q_lat.shape
    q = (rms_norm(q_lat, q_norm_g) @ w_uq).reshape(B_, S_, MLA_HEADS, MLA_NOPE + MLA_ROPE)
    q_nope, q_pe = q[..., :MLA_NOPE], rope(q[..., MLA_NOPE:], pos)
    kv = (rms_norm(kv_lat, kv_norm_g) @ w_ukv).reshape(B_, S_, MLA_HEADS, MLA_NOPE + MLA_V)
    k_nope, v = kv[..., :MLA_NOPE], kv[..., MLA_NOPE:]
    k_pe = rope(k_rope[:, :, None, :], pos)[:, :, 0]
    scale = (MLA_NOPE + MLA_ROPE) ** -0.5
    nqb = S_ // MLA_QBLOCK

    def blocks(t):
        return jnp.moveaxis(t.reshape(B_, nqb, MLA_QBLOCK, *t.shape[2:]), 1, 0)

    def attend(qb):
        qn, qr = qb
        s = jnp.einsum('bqhd,bkhd->bhqk', qn, k_nope) + jnp.einsum('bqhd,bkd->bhqk', qr, k_pe)
        p = jax.nn.softmax(s.astype(jnp.float32) * scale, axis=-1)
        return jnp.einsum('bhqk,bkhd->bqhd', p.astype(v.dtype), v)

    o = lax.map(attend, (blocks(q_nope), blocks(q_pe)))
    return jnp.moveaxis(o, 0, 1).reshape(B_, S_, W_A)


def dilated_group(q, k, v, window, dilation):
    B_, S_, H_, dh = q.shape
    r = dilation
    R = window // (2 * dilation)
    L = S_ // r
    QB = DIL_QBLOCK
    nb = -(-L // QB)
    Lp = nb * QB

    def to_sub(t):
        return t.reshape(B_, L, r, H_, dh).transpose(0, 2, 3, 1, 4)

    pad_q = ((0, 0), (0, 0), (0, 0), (0, Lp - L), (0, 0))
    pad_k = ((0, 0), (0, 0), (0, 0), (R, Lp - L + R), (0, 0))
    qs = jnp.pad(to_sub(q), pad_q)
    kp = jnp.pad(to_sub(k), pad_k)
    vp = jnp.pad(to_sub(v), pad_k)
    key_idx = np.arange(nb)[:, None] * QB + np.arange(QB + 2 * R)[None, :]
    kb = kp[:, :, :, key_idx]
    vb = vp[:, :, :, key_idx]
    qb = qs.reshape(B_, r, H_, nb, QB, dh)
    qpos = np.arange(nb)[:, None] * QB + np.arange(QB)[None, :]
    kpos = key_idx - R
    mask = ((np.abs(qpos[:, :, None] - kpos[:, None, :]) <= R)
            & (kpos[:, None, :] >= 0) & (kpos[:, None, :] < L))
    s = jnp.einsum('bmhnqd,bmhnkd->bmhnqk', qb, kb).astype(jnp.float32) * (dh ** -0.5)
    s = jnp.where(mask, s, NEG_INF)
    m = jnp.max(s, axis=-1, keepdims=True)
    p = jnp.exp(s - m)
    den = jnp.sum(p, axis=-1, keepdims=True)
    o = jnp.einsum('bmhnqk,bmhnkd->bmhnqd', p, vb.astype(jnp.float32)) / den
    lse = (m + jnp.log(den))[..., 0]
    o = o.reshape(B_, r, H_, Lp, dh)[:, :, :, :L].transpose(0, 3, 1, 2, 4).reshape(B_, S_, H_, dh)
    lse = lse.reshape(B_, r, H_, Lp)[..., :L].transpose(0, 3, 1, 2).reshape(B_, S_, H_)
    return o, lse


def dilated_attention(qkv, pos):
    B_, S_, _ = qkv.shape
    qkv = qkv.reshape(B_, S_, len(DIL_PATTERNS), 3, DIL_HEADS, DIL_DH)
    outs, lses = [], []
    for g, (window, dilation) in enumerate(DIL_PATTERNS):
        q = rope(qkv[:, :, g, 0], pos)
        k = rope(qkv[:, :, g, 1], pos)
        o, l = dilated_group(q, k, qkv[:, :, g, 2], window, dilation)
        outs.append(o)
        lses.append(l)
    w = jax.nn.softmax(jnp.stack(lses, axis=0), axis=0)
    o = jnp.sum(w[..., None] * jnp.stack(outs, axis=0), axis=0)
    return o.reshape(B_, S_, W_B).astype(qkv.dtype)


def neighbourhood_attention(q, k, v, rpb):
    B_, S_, _ = q.shape
    rows = S_ // GRID_W
    kh = min(NAT_KH_MAX, rows)

    def grid(t):
        return t.reshape(B_, rows, GRID_W, NAT_HEADS, NAT_DH).transpose(0, 3, 1, 2, 4)

    qg, kg, vg = grid(q), grid(k), grid(v)
    ncb = GRID_W // NAT_QCOL
    qcol = np.arange(GRID_W).reshape(ncb, NAT_QCOL)
    kcol_start = np.clip(np.arange(ncb) * NAT_QCOL - NAT_KW // 2, 0, GRID_W - NAT_KCOL)
    kcol = kcol_start[:, None] + np.arange(NAT_KCOL)[None, :]
    win_start = np.clip(qcol - NAT_KW // 2, 0, GRID_W - NAT_KW)
    col_mask = ((kcol[:, None, :] >= win_start[:, :, None])
                & (kcol[:, None, :] < win_start[:, :, None] + NAT_KW))
    col_off = np.clip(kcol[:, None, :] - qcol[:, :, None] + NAT_KW - 1, 0, 2 * NAT_KW - 2)
    rpb_cols = rpb[:, :, col_off]
    scale = NAT_DH ** -0.5

    def attend_row(r):
        rs = jnp.clip(r - kh // 2, 0, rows - kh)
        q_r = lax.dynamic_index_in_dim(qg, r, axis=2, keepdims=False)
        q_r = q_r.reshape(B_, NAT_HEADS, ncb, NAT_QCOL, NAT_DH)
        k_rows = lax.dynamic_slice_in_dim(kg, rs, kh, axis=2)[:, :, :, kcol]
        v_rows = lax.dynamic_slice_in_dim(vg, rs, kh, axis=2)[:, :, :, kcol]
        row_off = rs + jnp.arange(kh) - r + NAT_KH_MAX - 1
        bias = jnp.take(rpb_cols, row_off, axis=1).transpose(0, 2, 3, 1, 4)
        s = jnp.einsum('bhjqd,bhrjcd->bhjqrc', q_r, k_rows).astype(jnp.float32) * scale
        s = s + bias.astype(jnp.float32)[None]
        s = jnp.where(col_mask[None, None, :, :, None, :], s, NEG_INF)
        shp = s.shape
        p = jax.nn.softmax(s.reshape(*shp[:4], kh * NAT_KCOL), axis=-1).reshape(shp)
        o = jnp.einsum('bhjqrc,bhrjcd->bhjqd', p.astype(v_rows.dtype), v_rows)
        return o.reshape(B_, NAT_HEADS, GRID_W, NAT_DH)

    o = lax.map(attend_row, jnp.arange(rows))
    return o.transpose(1, 0, 3, 2, 4).reshape(B_, S_, W_C)


def gla_direction(q, k, v, log_a):
    B_, S_, H_, dk = q.shape
    dv = v.shape[-1]
    C = GLA_CHUNK
    n = S_ // C

    def ch(t):
        return t.reshape(B_, n, C, H_, t.shape[-1])

    q, k, v, log_a = ch(q), ch(k), ch(v), ch(log_a)
    b = jnp.cumsum(log_a, axis=2)
    b_last = b[:, :, -1]
    b_mid = b[:, :, C // 2 - 1][:, :, None]
    lower = np.tril(np.ones((C, C), dtype=bool))
    att = jnp.einsum('bnihd,bnjhd->bnhij', q * jnp.exp(b - b_mid), k * jnp.exp(b_mid - b))
    att = jnp.where(lower, att, 0.0)
    o = jnp.einsum('bnhij,bnjhe->bnihe', att, v)
    kv = jnp.einsum('bnjhd,bnjhe->bnhde', k * jnp.exp(b_last[:, :, None] - b), v)

    def step(state, inp):
        decay, kv_c = inp
        return decay[..., None] * state + kv_c, state

    _, states = lax.scan(step, jnp.zeros((B_, H_, dk, dv), jnp.float32),
                         (jnp.moveaxis(jnp.exp(b_last), 1, 0), jnp.moveaxis(kv, 1, 0)))
    o = o + jnp.einsum('bnihd,nbhde->bnihe', q * jnp.exp(b), states)
    return o.reshape(B_, S_, H_, dv)


def gla_attention(q, k, v, g_f, g_b, w_gf, b_gf, w_gb, b_gb, norm_g):
    B_, S_, _ = q.shape

    def heads(t, d):
        return t.astype(jnp.float32).reshape(B_, S_, GLA_HEADS, d)

    qh = heads(q, GLA_DK) * (GLA_DK ** -0.5)
    kh = heads(k, GLA_DK)
    vh = heads(v, GLA_DV)
    la_f = heads(jax.nn.log_sigmoid((g_f @ w_gf + b_gf).astype(jnp.float32)) / GLA_TAU, GLA_DK)
    la_b = heads(jax.nn.log_sigmoid((g_b @ w_gb + b_gb).astype(jnp.float32)) / GLA_TAU, GLA_DK)

    def flip(t):
        return jnp.flip(t, axis=1)

    o = gla_direction(qh, kh, vh, la_f) + flip(gla_direction(flip(qh), flip(kh), flip(vh), flip(la_b)))
    o = rms_norm(o, norm_g)
    return o.reshape(B_, S_, W_D).astype(q.dtype)


def hybrid_layer(x, pos, norm_g, w_in, q_norm_g, w_uq, kv_norm_g, w_ukv, rpb,
                 w_gf, b_gf, w_gb, b_gb, gla_norm_g, w_pa, w_pb, w_pc, w_pd,
                 w_merge, b_merge, w_out):
    B_, S_, _ = x.shape
    h = rms_norm(x, norm_g)
    (a_q, a_kv, a_kr, b_qkv, c_q, c_k, c_v, d_q, d_k, d_v, d_gf, d_gb,
     z_a, z_b, z_c, z_d) = jnp.split(h @ w_in, IN_SPLITS, axis=-1)
    o_a = mla_attention(a_q, a_kv, a_kr, q_norm_g, w_uq, kv_norm_g, w_ukv, pos) * jax.nn.silu(z_a)
    o_b = dilated_attention(b_qkv, pos) * jax.nn.silu(z_b)
    o_c = neighbourhood_attention(c_q, c_k, c_v, rpb) * jax.nn.silu(z_c)
    o_d = gla_attention(d_q, d_k, d_v, d_gf, d_gb, w_gf, b_gf, w_gb, b_gb, gla_norm_g) * jax.nn.silu(z_d)
    gates = jax.nn.sigmoid((h @ w_merge + b_merge).astype(jnp.float32)).astype(x.dtype)
    gates = gates.reshape(B_, S_, N_BRANCH, x.shape[-1])
    mixed = (gates[:, :, 0] * (o_a @ w_pa) + gates[:, :, 1] * (o_b @ w_pb)
             + gates[:, :, 2] * (o_c @ w_pc) + gates[:, :, 3] * (o_d @ w_pd))
    return x + mixed @ w_out


def setup_inputs(seed: int = 0) -> dict:
    key = jax.random.key(seed)
    ks = jax.random.split(key, 21)

    def nrm(k, shape, scale):
        return jax.random.normal(k, shape, jnp.float32) * scale

    def gain(k, shape):
        return 1.0 + 0.02 * jax.random.normal(k, shape, jnp.float32)

    L_, D = DEPTH, D_MODEL
    return {
        'x': nrm(ks[0], (BATCH, SEQ, D), 1.0),
        'norm_g': gain(ks[1], (L_, D)),
        'w_in': nrm(ks[2], (L_, D, D_IN), D ** -0.5),
        'mla_q_norm_g': gain(ks[3], (L_, MLA_Q_RANK)),
        'mla_w_uq': nrm(ks[4], (L_, MLA_Q_RANK, MLA_HEADS * (MLA_NOPE + MLA_ROPE)), MLA_Q_RANK ** -0.5),
        'mla_kv_norm_g': gain(ks[5], (L_, MLA_KV_RANK)),
        'mla_w_ukv': nrm(ks[6], (L_, MLA_KV_RANK, MLA_HEADS * (MLA_NOPE + MLA_V)), MLA_KV_RANK ** -0.5),
        'nat_rpb': nrm(ks[7], (L_, NAT_HEADS, 2 * NAT_KH_MAX - 1, 2 * NAT_KW - 1), 0.1),
        'gla_w_gate_f': nrm(ks[8], (L_, GLA_GATE_RANK, GLA_HEADS * GLA_DK), GLA_GATE_RANK ** -0.5),
        'gla_b_gate_f': nrm(ks[9], (L_, GLA_HEADS * GLA_DK), 0.1),
        'gla_w_gate_b': nrm(ks[10], (L_, GLA_GATE_RANK, GLA_HEADS * GLA_DK), GLA_GATE_RANK ** -0.5),
        'gla_b_gate_b': nrm(ks[11], (L_, GLA_HEADS * GLA_DK), 0.1),
        'gla_norm_g': gain(ks[12], (L_, GLA_HEADS, GLA_DV)),
        'w_proj_a': nrm(ks[13], (L_, W_A, D), W_A ** -0.5),
        'w_proj_b': nrm(ks[14], (L_, W_B, D), W_B ** -0.5),
        'w_proj_c': nrm(ks[15], (L_, W_C, D), W_C ** -0.5),
        'w_proj_d': nrm(ks[16], (L_, W_D, D), W_D ** -0.5),
        'w_merge': nrm(ks[17], (L_, D, N_BRANCH * D), D ** -0.5),
        'b_merge': nrm(ks[18], (L_, N_BRANCH * D), 0.02),
        'w_out': nrm(ks[19], (L_, D, D), D ** -0.5),
        'final_norm_g': gain(ks[20], (D,)),
    }


def reference(x, norm_g, w_in, mla_q_norm_g, mla_w_uq, mla_kv_norm_g, mla_w_ukv, nat_rpb,
              gla_w_gate_f, gla_b_gate_f, gla_w_gate_b, gla_b_gate_b, gla_norm_g,
              w_proj_a, w_proj_b, w_proj_c, w_proj_d, w_merge, b_merge, w_out, final_norm_g):
    pos = jnp.arange(x.shape[1], dtype=jnp.int32)
    for l in range(DEPTH):
        x = hybrid_layer(x, pos, norm_g[l], w_in[l], mla_q_norm_g[l], mla_w_uq[l],
                         mla_kv_norm_g[l], mla_w_ukv[l], nat_rpb[l],
                         gla_w_gate_f[l], gla_b_gate_f[l], gla_w_gate_b[l], gla_b_gate_b[l],
                         gla_norm_g[l], w_proj_a[l], w_proj_b[l], w_proj_c[l], w_proj_d[l],
                         w_merge[l], b_merge[l], w_out[l])
    return rms_norm(x, final_norm_g)
```

```python
import functools

import numpy as np
import jax
import jax.numpy as jnp
from jax import lax
from jax.experimental import pallas as pl
from jax.experimental.pallas import tpu as pltpu

F32 = jnp.float32
BF16 = jnp.bfloat16

GRID_W = 64
ROPE_THETA = 10000.0
NORM_EPS = 1e-6
NEG_INF = -1e30

MLA_HEADS = 4
MLA_NOPE = 128
MLA_ROPE = 64
MLA_V = 128
MLA_Q_RANK = 256
MLA_KV_RANK = 128
DIL_PATTERNS = ((128, 1), (512, 4), (2048, 16))
DIL_HEADS = 4
DIL_DH = 64
NAT_HEADS = 8
NAT_DH = 64
NAT_KH_MAX = 8
NAT_KW = 16
GLA_HEADS = 4
GLA_DK = 64
GLA_DV = 128
GLA_GATE_RANK = 16
GLA_TAU = 16.0
GLA_CHUNK = 64

W_A = MLA_HEADS * MLA_V
W_B = DIL_HEADS * DIL_DH
W_C = NAT_HEADS * NAT_DH
W_D = GLA_HEADS * GLA_DV
N_BRANCH = 4
W_DIL_GROUP = 3 * W_B
W_GLA_QK = GLA_HEADS * GLA_DK

LANES = 128
MLA_HEAD_STRIDE = 256
VMEM_LIMIT = 56 * 1024 * 1024

_IN_SIZES = (MLA_Q_RANK, MLA_KV_RANK, MLA_ROPE, 3 * W_DIL_GROUP, W_C, W_C, W_C,
             W_GLA_QK, W_GLA_QK, W_D, GLA_GATE_RANK, GLA_GATE_RANK, W_A, W_B, W_C, W_D)
_IN_OFF = tuple(int(v) for v in np.concatenate([[0], np.cumsum(_IN_SIZES)]))

_P_Q = 0
_P_KV = _P_Q + MLA_Q_RANK
_P_KR = _P_KV + MLA_KV_RANK
_P_DIL = _P_KR + LANES
_P_NAT = _P_DIL + 3 * W_DIL_GROUP
_P_GQ = _P_NAT + 3 * W_C
_P_GK = _P_GQ + W_GLA_QK
_P_GV = _P_GK + W_GLA_QK
_P_GG = _P_GV + W_D
_P_END = _P_GG + LANES


def _dot(a, b):
    return jnp.dot(a, b, preferred_element_type=F32)


def _dot_nt(a, b):
    return lax.dot_general(a, b, (((1,), (1,)), ((), ())), preferred_element_type=F32)


def _rms(x, g):
    return x * lax.rsqrt(jnp.mean(x * x, axis=-1, keepdims=True) + NORM_EPS) * g


def _rope128(x, cos, sin_hi, sin_lo):
    return x * cos + pltpu.roll(x, 32, 1) * sin_hi + pltpu.roll(x, LANES - 32, 1) * sin_lo


def _sigmoid(x):
    return 1.0 / (1.0 + jnp.exp(-x))


def _log_sigmoid(x):
    return jnp.minimum(x, 0.0) - jnp.log1p(jnp.exp(-jnp.abs(x)))


def _params(*sem):
    return pltpu.CompilerParams(dimension_semantics=sem, vmem_limit_bytes=VMEM_LIMIT)


def _const_spec(shape):
    nd = len(shape)
    return pl.BlockSpec(shape, lambda *_: (0,) * nd)


def _in_proj_kernel(x_ref, cos_ref, shi_ref, slo_ref, g_ref, w_ref, qg_ref, wuq_ref, kvg_ref,
                    wukv_ref, wgate_ref, bgate_ref,
                    q_ref, kc_ref, va_ref, d0_ref, d1_ref, d2_ref, nat_ref, gq_ref, gk_ref,
                    gv_ref, la_ref):
    hb = _rms(x_ref[...], g_ref[...]).astype(BF16)
    cos, shi, slo = cos_ref[...], shi_ref[...], slo_ref[...]

    def mm(a, width):
        return _dot(hb, w_ref[:, a:a + width])

    def rope(t):
        return _rope128(t, cos, shi, slo)

    qn = _rms(mm(_P_Q, MLA_Q_RANK), qg_ref[...]).astype(BF16)
    for h in range(MLA_HEADS):
        base = h * MLA_HEAD_STRIDE
        qh = _dot(qn, wuq_ref[:, base:base + MLA_HEAD_STRIDE])
        q_ref[:, base:base + LANES] = qh[:, :LANES].astype(BF16)
        q_ref[:, base + LANES:base + 2 * LANES] = rope(qh[:, LANES:]).astype(BF16)
    kvn = _rms(mm(_P_KV, MLA_KV_RANK), kvg_ref[...]).astype(BF16)
    kpe = rope(mm(_P_KR, LANES)).astype(BF16)
    for h in range(MLA_HEADS):
        base = h * MLA_HEAD_STRIDE
        kc_ref[:, base:base + LANES] = _dot(kvn, wukv_ref[:, h * LANES:(h + 1) * LANES]).astype(BF16)
        kc_ref[:, base + LANES:base + 2 * LANES] = kpe
    va_ref[...] = _dot(kvn, wukv_ref[:, W_A:2 * W_A]).astype(BF16)

    for g, d_ref in enumerate((d0_ref, d1_ref, d2_ref)):
        base = _P_DIL + g * W_DIL_GROUP
        for c in range(0, 2 * W_B, LANES):
            d_ref[:, c:c + LANES] = rope(mm(base + c, LANES)).astype(BF16)
        d_ref[:, 2 * W_B:] = mm(base + 2 * W_B, W_B).astype(BF16)

    for c in range(0, 3 * W_C, W_C):
        nat_ref[:, c:c + W_C] = mm(_P_NAT + c, W_C).astype(BF16)

    gq_ref[...] = mm(_P_GQ, W_GLA_QK) * (GLA_DK ** -0.5)
    gk_ref[...] = mm(_P_GK, W_GLA_QK)
    gv_ref[...] = mm(_P_GV, W_D).astype(BF16)
    gates = _dot(mm(_P_GG, LANES).astype(BF16), wgate_ref[...]) + bgate_ref[...]
    la_ref[...] = _log_sigmoid(gates) * (1.0 / GLA_TAU)


def _in_proj(x2, seq, tabs, g, w, qg, wuq, kvg, wukv, wgate, bgate, tm):
    n, d = x2.shape
    nseq = seq // tm
    row = lambda width: pl.BlockSpec((tm, width), lambda i: (i, 0))
    tab = pl.BlockSpec((tm, LANES), lambda i: (i % nseq, 0))
    outs = [(4 * MLA_HEAD_STRIDE, BF16), (4 * MLA_HEAD_STRIDE, BF16), (W_A, BF16),
            (W_DIL_GROUP, BF16), (W_DIL_GROUP, BF16), (W_DIL_GROUP, BF16), (3 * W_C, BF16),
            (W_GLA_QK, F32), (W_GLA_QK, F32), (W_D, BF16), (2 * W_GLA_QK, F32)]
    return pl.pallas_call(
        _in_proj_kernel,
        grid=(n // tm,),
        in_specs=[row(d), tab, tab, tab] + [_const_spec(a.shape) for a in
                                             (g, w, qg, wuq, kvg, wukv, wgate, bgate)],
        out_specs=[row(wd) for wd, _ in outs],
        out_shape=[jax.ShapeDtypeStruct((n, wd), dt) for wd, dt in outs],
        compiler_params=_params("parallel"),
        name="in_proj",
    )(x2, *tabs, g, w, qg, wuq, kvg, wukv, wgate, bgate)


def _mla_kernel(q_ref, k_ref, v_ref, o_ref):
    scale = (MLA_NOPE + MLA_ROPE) ** -0.5
    s = _dot_nt(q_ref[0], k_ref[0]) * scale
    m = jnp.max(s, axis=-1, keepdims=True)
    p = jnp.exp(s - m)
    den = jnp.sum(p, axis=-1, keepdims=True)
    o_ref[0] = (_dot(p.astype(BF16), v_ref[0]) / den).astype(o_ref.dtype)


def _mla(q, kc, va, tq):
    b, s, _ = q.shape
    return pl.pallas_call(
        _mla_kernel,
        grid=(b, MLA_HEADS, s // tq),
        in_specs=[pl.BlockSpec((1, tq, MLA_HEAD_STRIDE), lambda i, h, j: (i, j, h)),
                  pl.BlockSpec((1, s, MLA_HEAD_STRIDE), lambda i, h, j: (i, 0, h)),
                  pl.BlockSpec((1, s, MLA_V), lambda i, h, j: (i, 0, h))],
        out_specs=pl.BlockSpec((1, tq, MLA_V), lambda i, h, j: (i, j, h)),
        out_shape=jax.ShapeDtypeStruct((b, s, W_A), BF16),
        compiler_params=_params("parallel", "parallel", "parallel"),
        name="mla_attention",
    )(q, kc, va)


def _head_pair_attend(q128, k128, v128, bias_fn, scale):
    lane = lax.broadcasted_iota(jnp.int32, q128.shape, 1)
    outs, lses = [], []
    for hh in range(2):
        sel = (lane < 64) if hh == 0 else (lane >= 64)
        qm = jnp.where(sel, q128, jnp.zeros_like(q128))
        s = bias_fn(hh, _dot_nt(qm, k128) * scale)
        m = jnp.max(s, axis=-1, keepdims=True)
        p = jnp.exp(s - m)
        den = jnp.sum(p, axis=-1, keepdims=True)
        outs.append(_dot(p.astype(BF16), v128) / den)
        lses.append(m + jnp.log(den))
    olane = lax.broadcasted_iota(jnp.int32, outs[0].shape, 1)
    o = jnp.where(olane < 64, outs[0], outs[1])
    lse = jnp.where(olane < 64, lses[0], lses[1])
    return o, lse


def _dil_kernel(qkv_ref, o_ref, lse_ref, *, length, qb, win, radius):
    nblk = length // qb

    def body(n, carry):
        q0 = pl.multiple_of(n * qb, qb)
        ks = pl.multiple_of(jnp.clip(q0 - radius, 0, length - win), 64)
        qpos = q0 + lax.broadcasted_iota(jnp.int32, (qb, win), 0)
        kpos = ks + lax.broadcasted_iota(jnp.int32, (qb, win), 1)
        mask = jnp.abs(qpos - kpos) <= radius

        def bias_fn(_, s):
            return jnp.where(mask, s, NEG_INF)

        for pr in range(DIL_HEADS // 2):
            c = pr * LANES
            q128 = qkv_ref[0, pl.ds(q0, qb), c:c + LANES]
            k128 = qkv_ref[0, pl.ds(ks, win), W_B + c:W_B + c + LANES]
            v128 = qkv_ref[0, pl.ds(ks, win), 2 * W_B + c:2 * W_B + c + LANES]
            o, lse = _head_pair_attend(q128, k128, v128, bias_fn, DIL_DH ** -0.5)
            o_ref[0, pl.ds(q0, qb), c:c + LANES] = o.astype(o_ref.dtype)
            lse_ref[0, pl.ds(q0, qb), c:c + LANES] = lse
        return carry

    lax.fori_loop(0, nblk, body, 0)


def _dilated_group(qkv, window, dilation):
    b, s, _ = qkv.shape
    r = dilation
    radius = window // (2 * dilation)
    length = s // r
    qb = min(128, length)
    win = min(qb + 2 * radius, length)
    view = qkv.reshape(b, length, r * W_DIL_GROUP)
    o, lse = pl.pallas_call(
        functools.partial(_dil_kernel, length=length, qb=qb, win=win, radius=radius),
        grid=(b, r),
        in_specs=[pl.BlockSpec((1, length, W_DIL_GROUP), lambda i, m: (i, 0, m))],
        out_specs=[pl.BlockSpec((1, length, W_B), lambda i, m: (i, 0, m)),
                   pl.BlockSpec((1, length, W_B), lambda i, m: (i, 0, m))],
        out_shape=[jax.ShapeDtypeStruct((b, length, r * W_B), BF16),
                   jax.ShapeDtypeStruct((b, length, r * W_B), F32)],
        compiler_params=_params("parallel", "parallel"),
        name=f"dilated_attention_r{r}",
    )(view)
    return o.reshape(b, s, W_B), lse.reshape(b, s, W_B)


def _nat_bias_table(rpb, kh):
    qc = np.arange(GRID_W)[:, None]
    kc = np.arange(GRID_W)[None, :]
    win_start = np.clip(qc - NAT_KW // 2, 0, GRID_W - NAT_KW)
    allowed = (kc >= win_start) & (kc < win_start + NAT_KW)
    col_off = np.clip(kc - qc + NAT_KW - 1, 0, 2 * NAT_KW - 2)
    n_var = NAT_KH_MAX
    row_off = np.clip(np.arange(n_var)[:, None] + np.arange(kh)[None, :], 0, 2 * NAT_KH_MAX - 2)
    t = rpb[:, row_off]
    t = t[:, :, :, col_off]
    t = jnp.where(allowed[None, None, None], t, NEG_INF)
    t = t.transpose(0, 1, 3, 2, 4)
    return t.reshape(NAT_HEADS, n_var, GRID_W, kh * GRID_W).astype(F32)


def _nat_kernel(q_ref, k_ref, v_ref, bias_ref, o_ref, *, rows, kh):
    def body(r, carry):
        rs = jnp.clip(r - kh // 2, 0, rows - kh)
        var = rs - r + NAT_KH_MAX - 1
        q0 = pl.multiple_of(r * GRID_W, GRID_W)
        k0 = pl.multiple_of(rs * GRID_W, GRID_W)
        q128 = q_ref[0, pl.ds(q0, GRID_W), :]
        k128 = k_ref[0, pl.ds(k0, kh * GRID_W), :]
        v128 = v_ref[0, pl.ds(k0, kh * GRID_W), :]

        def bias_fn(hh, s):
            return s + bias_ref[hh, var]

        o, _ = _head_pair_attend(q128, k128, v128, bias_fn, NAT_DH ** -0.5)
        o_ref[0, pl.ds(q0, GRID_W), :] = o.astype(o_ref.dtype)
        return carry

    lax.fori_loop(0, rows, body, 0)


def _nat(qkv, rpb):
    b, s, _ = qkv.shape
    rows = s // GRID_W
    kh = min(NAT_KH_MAX, rows)
    table = _nat_bias_table(rpb, kh)
    npair = NAT_HEADS // 2
    col = lambda off: pl.BlockSpec((1, s, LANES), lambda i, p: (i, 0, off + p))
    return pl.pallas_call(
        functools.partial(_nat_kernel, rows=rows, kh=kh),
        grid=(b, npair),
        in_specs=[col(0), col(npair), col(2 * npair),
                  pl.BlockSpec((2, NAT_KH_MAX, GRID_W, kh * GRID_W), lambda i, p: (p, 0, 0, 0))],
        out_specs=pl.BlockSpec((1, s, LANES), lambda i, p: (i, 0, p)),
        out_shape=jax.ShapeDtypeStruct((b, s, W_C), BF16),
        compiler_params=_params("parallel", "parallel"),
        name="neighbourhood_attention",
    )(qkv, qkv, qkv, table)


def _split3(x):
    hi = x.astype(BF16)
    r1 = x - hi.astype(F32)
    mid = r1.astype(BF16)
    lo = (r1 - mid.astype(F32)).astype(BF16)
    return hi, mid, lo


def _gla_kernel(q_ref, k_ref, v_ref, laf_ref, lab_ref, g_ref, o_ref, acc_ref, st_ref, *, seq):
    c = GLA_CHUNK
    nchunk = seq // c
    ri = lax.broadcasted_iota(jnp.int32, (c, c), 0)
    ci = lax.broadcasted_iota(jnp.int32, (c, c), 1)
    lane = lax.broadcasted_iota(jnp.int32, (c, LANES), 1)
    head_sel = (lane < GLA_DK, lane >= GLA_DK)

    def direction(la_ref, reverse):
        keep = (ci >= ri) if reverse else (ci <= ri)
        tri = keep.astype(BF16)
        total_row, mid_row = (0, c // 2) if reverse else (c - 1, c // 2 - 1)
        st_ref[...] = jnp.zeros_like(st_ref)

        def body(n, carry):
            chunk = (nchunk - 1 - n) if reverse else n
            r0 = pl.multiple_of(chunk * c, c)
            q = q_ref[0, pl.ds(r0, c), :]
            k = k_ref[0, pl.ds(r0, c), :]
            la = la_ref[0, pl.ds(r0, c), :]
            hi, mid, lo = _split3(la)
            bcum = _dot(tri, hi) + _dot(tri, mid) + _dot(tri, lo)
            b_tot = bcum[total_row:total_row + 1, :]
            b_mid = bcum[mid_row:mid_row + 1, :]
            q_in = q * jnp.exp(bcum - b_mid)
            k_in = (k * jnp.exp(b_mid - bcum)).astype(BF16)
            k_dec = (k * jnp.exp(b_tot - bcum)).astype(BF16)
            q_st = q * jnp.exp(bcum)
            decay = jnp.exp(b_tot)
            for hh in range(2):
                v_h = v_ref[0, pl.ds(r0, c), hh * GLA_DV:(hh + 1) * GLA_DV]
                zero = jnp.zeros_like(q)
                att = _dot_nt(jnp.where(head_sel[hh], q_in, zero).astype(BF16), k_in)
                att = jnp.where(keep, att, 0.0)
                state = st_ref[hh]
                o = _dot(att.astype(BF16), v_h)
                o = o + _dot_nt(jnp.where(head_sel[hh], q_st, zero).astype(BF16), state.astype(BF16))
                sl = (pl.ds(r0, c), slice(hh * GLA_DV, (hh + 1) * GLA_DV))
                if reverse:
                    acc_ref[sl] = acc_ref[sl] + o
                else:
                    acc_ref[sl] = o
                kv_t = _dot(v_h.astype(F32).T.astype(BF16), k_dec)
                st_ref[hh] = state * decay + kv_t
            return carry

        lax.fori_loop(0, nchunk, body, 0)

    direction(laf_ref, False)
    direction(lab_ref, True)
    for hh in range(2):
        sl = slice(hh * GLA_DV, (hh + 1) * GLA_DV)
        o_ref[0, :, sl] = _rms(acc_ref[:, sl], g_ref[:, sl]).astype(o_ref.dtype)


def _gla(gq, gk, gv, la, gain):
    b, s, _ = gq.shape
    npair = GLA_HEADS // 2
    qk = pl.BlockSpec((1, s, LANES), lambda i, p: (i, 0, p))
    return pl.pallas_call(
        functools.partial(_gla_kernel, seq=s),
        grid=(b, npair),
        in_specs=[qk, qk,
                  pl.BlockSpec((1, s, 2 * GLA_DV), lambda i, p: (i, 0, p)),
                  pl.BlockSpec((1, s, LANES), lambda i, p: (i, 0, p)),
                  pl.BlockSpec((1, s, LANES), lambda i, p: (i, 0, npair + p)),
                  pl.BlockSpec((1, 2 * GLA_DV), lambda i, p: (0, p))],
        out_specs=pl.BlockSpec((1, s, 2 * GLA_DV), lambda i, p: (i, 0, p)),
        out_shape=jax.ShapeDtypeStruct((b, s, W_D), BF16),
        scratch_shapes=[pltpu.VMEM((s, 2 * GLA_DV), F32), pltpu.VMEM((2, GLA_DV, LANES), F32)],
        compiler_params=_params("parallel", "parallel"),
        name="gated_linear_attention",
    )(gq, gk, gv, la, la, gain)


def _merge_kernel(x_ref, g_ref, oa_ref, ob0_ref, ob1_ref, ob2_ref, l0_ref, l1_ref, l2_ref,
                  oc_ref, od_ref, wz_ref, wm_ref, bm_ref, wp_ref, wo_ref, fg_ref, y_ref,
                  *, final):
    x = x_ref[...]
    d = x.shape[-1]
    hb = _rms(x, g_ref[...]).astype(BF16)

    lses = (l0_ref[...], l1_ref[...], l2_ref[...])
    mx = jnp.maximum(jnp.maximum(lses[0], lses[1]), lses[2])
    es = [jnp.exp(l - mx) for l in lses]
    tot = es[0] + es[1] + es[2]
    o_b = sum(e / tot * o[...].astype(F32) for e, o in zip(es, (ob0_ref, ob1_ref, ob2_ref)))

    branches = ((oa_ref[...].astype(F32), W_A), (o_b, W_B), (oc_ref[...].astype(F32), W_C),
                (od_ref[...].astype(F32), W_D))
    mixed = jnp.zeros(x.shape, F32)
    off = 0
    for i, (o, width) in enumerate(branches):
        z = _dot(hb, wz_ref[:, off:off + width])
        u = (o * (z * _sigmoid(z))).astype(BF16)
        proj = _dot(u, wp_ref[off:off + width, :])
        gate = _sigmoid(_dot(hb, wm_ref[:, i * d:(i + 1) * d]) + bm_ref[:, i * d:(i + 1) * d])
        mixed = mixed + gate * proj
        off += width
    y = x + _dot(mixed.astype(BF16), wo_ref[...])
    if final:
        y = _rms(y, fg_ref[...])
    y_ref[...] = y


def _merge(x2, g, oa, obs, lses, oc, od, wz, wm, bm, wp, wo, fg, final, tm):
    n, d = x2.shape
    row = lambda width: pl.BlockSpec((tm, width), lambda i: (i, 0))
    consts = (wz, wm, bm, wp, wo, fg)
    return pl.pallas_call(
        functools.partial(_merge_kernel, final=final),
        grid=(n // tm,),
        in_specs=[row(d), _const_spec(g.shape), row(W_A)] + [row(W_B)] * 6 + [row(W_C), row(W_D)]
                 + [_const_spec(a.shape) for a in consts],
        out_specs=row(d),
        out_shape=jax.ShapeDtypeStruct((n, d), F32),
        compiler_params=_params("parallel"),
        name="gate_merge_out",
    )(x2, g, oa, *obs, *lses, oc, od, *consts)


def _rope_tables(seq):
    half = MLA_ROPE // 2
    inv = jnp.power(ROPE_THETA, -jnp.arange(0, MLA_ROPE, 2, dtype=F32) / MLA_ROPE)
    ang = jnp.arange(seq, dtype=jnp.int32).astype(F32)[:, None] * inv[None, :]
    cos, sin = jnp.cos(ang), jnp.sin(ang)
    zero = jnp.zeros((seq, half), F32)
    cos_t = jnp.concatenate([cos] * 4, axis=-1)
    sin_hi = jnp.concatenate([zero, sin, zero, sin], axis=-1)
    sin_lo = jnp.concatenate([-sin, zero, -sin, zero], axis=-1)
    return cos_t, sin_hi, sin_lo


def _prep_layer(w_in, w_uq, w_ukv, w_gf, b_gf, w_gb, b_gb):
    d = w_in.shape[0]
    o = _IN_OFF
    zpad = lambda width: jnp.zeros((d, width), w_in.dtype)
    w_pad = jnp.concatenate([
        w_in[:, o[0]:o[3]], zpad(LANES - MLA_ROPE),
        w_in[:, o[3]:o[10]],
        w_in[:, o[10]:o[12]], zpad(LANES - 2 * GLA_GATE_RANK)], axis=1).astype(BF16)
    w_z = w_in[:, o[12]:o[16]].astype(BF16)
    uq = w_uq.reshape(MLA_Q_RANK, MLA_HEADS, MLA_NOPE + MLA_ROPE)
    uq = jnp.pad(uq, ((0, 0), (0, 0), (0, MLA_HEAD_STRIDE - MLA_NOPE - MLA_ROPE)))
    uq = uq.reshape(MLA_Q_RANK, MLA_HEADS * MLA_HEAD_STRIDE).astype(BF16)
    ukv = w_ukv.reshape(MLA_KV_RANK, MLA_HEADS, MLA_NOPE + MLA_V)
    ukv = jnp.concatenate([ukv[:, :, :MLA_NOPE].reshape(MLA_KV_RANK, -1),
                           ukv[:, :, MLA_NOPE:].reshape(MLA_KV_RANK, -1)], axis=1).astype(BF16)
    wgate = jnp.zeros((LANES, 2 * W_GLA_QK), F32)
    wgate = wgate.at[:GLA_GATE_RANK, :W_GLA_QK].set(w_gf)
    wgate = wgate.at[GLA_GATE_RANK:2 * GLA_GATE_RANK, W_GLA_QK:].set(w_gb)
    bgate = jnp.concatenate([b_gf, b_gb])[None, :]
    return w_pad, w_z, uq, ukv, wgate.astype(BF16), bgate


def kernel(x, norm_g, w_in, mla_q_norm_g, mla_w_uq, mla_kv_norm_g, mla_w_ukv, nat_rpb,
           gla_w_gate_f, gla_b_gate_f, gla_w_gate_b, gla_b_gate_b, gla_norm_g,
           w_proj_a, w_proj_b, w_proj_c, w_proj_d, w_merge, b_merge, w_out, final_norm_g):
    b, s, d = x.shape
    depth = w_in.shape[0]
    tm = min(256, s)
    tq = min(256, s)
    tabs = _rope_tables(s)
    x2 = x.reshape(b * s, d)
    for l in range(depth):
        w_pad, w_z, uq, ukv, wgate, bgate = _prep_layer(
            w_in[l], mla_w_uq[l], mla_w_ukv[l], gla_w_gate_f[l], gla_b_gate_f[l],
            gla_w_gate_b[l], gla_b_gate_b[l])
        (q, kc, va, d0, d1, d2, nat_qkv, gq, gk, gv, la) = _in_proj(
            x2, s, tabs, norm_g[l][None, :], w_pad, mla_q_norm_g[l][None, :], uq,
            mla_kv_norm_g[l][None, :], ukv, wgate, bgate, tm)
        sh = lambda t: t.reshape(b, s, t.shape[-1])
        o_a = _mla(sh(q), sh(kc), sh(va), tq)
        obs, lses = [], []
        for dil_qkv, (window, dilation) in zip((d0, d1, d2), DIL_PATTERNS):
            o_g, lse_g = _dilated_group(sh(dil_qkv), window, dilation)
            obs.append(o_g.reshape(b * s, W_B))
            lses.append(lse_g.reshape(b * s, W_B))
        o_c = _nat(sh(nat_qkv), nat_rpb[l])
        o_d = _gla(sh(gq), sh(gk), sh(gv), sh(la), gla_norm_g[l].reshape(1, W_D))
        w_p = jnp.concatenate([w_proj_a[l], w_proj_b[l], w_proj_c[l], w_proj_d[l]], axis=0)
        x2 = _merge(x2, norm_g[l][None, :], o_a.reshape(b * s, W_A), obs, lses,
                    o_c.reshape(b * s, W_C), o_d.reshape(b * s, W_D), w_z,
                    w_merge[l].astype(BF16), b_merge[l][None, :], w_p.astype(BF16),
                    w_out[l].astype(BF16), final_norm_g[None, :], l == depth - 1, tm)
    return x2.reshape(b, s, d)
```

```python
import functools

import numpy as np
import jax
import jax.numpy as jnp
from jax import lax
from jax.experimental import pallas as pl
from jax.experimental.pallas import tpu as pltpu

F32 = jnp.float32
BF16 = jnp.bfloat16

GRID_W = 64
ROPE_THETA = 10000.0
NORM_EPS = 1e-6
NEG_INF = -1e30

MLA_HEADS = 4
MLA_NOPE = 128
MLA_ROPE = 64
MLA_V = 128
MLA_Q_RANK = 256
MLA_KV_RANK = 128
DIL_PATTERNS = ((128, 1), (512, 4), (2048, 16))
DIL_HEADS = 4
DIL_DH = 64
NAT_HEADS = 8
NAT_DH = 64
NAT_KH_MAX = 8
NAT_KW = 16
GLA_HEADS = 4
GLA_DK = 64
GLA_DV = 128
GLA_GATE_RANK = 16
GLA_TAU = 16.0
GLA_CHUNK = 64

W_A = MLA_HEADS * MLA_V
W_B = DIL_HEADS * DIL_DH
W_C = NAT_HEADS * NAT_DH
W_D = GLA_HEADS * GLA_DV
N_BRANCH = 4
W_DIL_GROUP = 3 * W_B
W_GLA_QK = GLA_HEADS * GLA_DK

LOG2_E = 1.4426950408889634
LANES = 128
MLA_HEAD_STRIDE = 256
MLA_SUB_BLOCKS = 2
VMEM_LIMIT = 56 * 1024 * 1024

_IN_SIZES = (MLA_Q_RANK, MLA_KV_RANK, MLA_ROPE, 3 * W_DIL_GROUP, W_C, W_C, W_C,
             W_GLA_QK, W_GLA_QK, W_D, GLA_GATE_RANK, GLA_GATE_RANK, W_A, W_B, W_C, W_D)
_IN_OFF = tuple(int(v) for v in np.concatenate([[0], np.cumsum(_IN_SIZES)]))

_P_Q = 0
_P_KV = _P_Q + MLA_Q_RANK
_P_KR = _P_KV + MLA_KV_RANK
_P_DIL = _P_KR + LANES
_P_NAT = _P_DIL + 3 * W_DIL_GROUP
_P_GQ = _P_NAT + 3 * W_C
_P_GK = _P_GQ + W_GLA_QK
_P_GV = _P_GK + W_GLA_QK
_P_GG = _P_GV + W_D
_P_END = _P_GG + LANES


def _dot(a, b):
    return jnp.dot(a, b, preferred_element_type=F32)


def _dot_nt(a, b):
    return lax.dot_general(a, b, (((1,), (1,)), ((), ())), preferred_element_type=F32)


def _rms(x, g):
    return x * lax.rsqrt(jnp.mean(x * x, axis=-1, keepdims=True) + NORM_EPS) * g


def _rope128(x, cos, sin_hi, sin_lo):
    return x * cos + pltpu.roll(x, 32, 1) * sin_hi + pltpu.roll(x, LANES - 32, 1) * sin_lo


def _sigmoid(x):
    return 1.0 / (1.0 + jnp.exp(-x))


def _log_sigmoid(x):
    return jnp.minimum(x, 0.0) - jnp.log1p(jnp.exp(-jnp.abs(x)))


def _params(*sem):
    return pltpu.CompilerParams(dimension_semantics=sem, vmem_limit_bytes=VMEM_LIMIT)


def _const_spec(shape):
    nd = len(shape)
    return pl.BlockSpec(shape, lambda *_: (0,) * nd, pipeline_mode=pl.Buffered(1))


def _in_proj_kernel(x_ref, cos_ref, shi_ref, slo_ref, g_ref, w_ref, qg_ref, wuq_ref, kvg_ref,
                    wukv_ref, wgate_ref, bgate_ref,
                    q_ref, kc_ref, va_ref, d0_ref, d1_ref, d2_ref, nat_ref, gq_ref, gk_ref,
                    gv_ref, la_ref):
    hb = _rms(x_ref[...], g_ref[...]).astype(BF16)
    cos, shi, slo = cos_ref[...], shi_ref[...], slo_ref[...]

    def mm(a, width):
        return _dot(hb, w_ref[:, a:a + width])

    def rope(t):
        return _rope128(t, cos, shi, slo)

    qn = _rms(mm(_P_Q, MLA_Q_RANK), qg_ref[...]).astype(BF16)
    for h in range(MLA_HEADS):
        base = h * MLA_HEAD_STRIDE
        qh = _dot(qn, wuq_ref[:, base:base + MLA_HEAD_STRIDE])
        q_ref[:, base:base + LANES] = qh[:, :LANES].astype(BF16)
        q_ref[:, base + LANES:base + 2 * LANES] = rope(qh[:, LANES:]).astype(BF16)
    kv_kr = mm(_P_KV, MLA_KV_RANK + LANES)
    kvn = _rms(kv_kr[:, :MLA_KV_RANK], kvg_ref[...]).astype(BF16)
    kpe = rope(kv_kr[:, MLA_KV_RANK:]).astype(BF16)
    k_nope = _dot(kvn, wukv_ref[:, :W_A])
    for h in range(MLA_HEADS):
        base = h * MLA_HEAD_STRIDE
        kc_ref[:, base:base + LANES] = k_nope[:, h * LANES:(h + 1) * LANES].astype(BF16)
        kc_ref[:, base + LANES:base + 2 * LANES] = kpe
    va_ref[...] = _dot(kvn, wukv_ref[:, W_A:2 * W_A]).astype(BF16)

    for g, d_ref in enumerate((d0_ref, d1_ref, d2_ref)):
        base = _P_DIL + g * W_DIL_GROUP
        qk = mm(base, 2 * W_B)
        for c in range(0, 2 * W_B, LANES):
            fold = DIL_DH ** -0.5 if c < W_B else 1.0
            d_ref[:, c:c + LANES] = (rope(qk[:, c:c + LANES]) * fold).astype(BF16)
        d_ref[:, 2 * W_B:] = mm(base + 2 * W_B, W_B).astype(BF16)

    nat_ref[:, :W_C] = (mm(_P_NAT, W_C) * (NAT_DH ** -0.5)).astype(BF16)
    for c in range(W_C, 3 * W_C, W_C):
        nat_ref[:, c:c + W_C] = mm(_P_NAT + c, W_C).astype(BF16)

    gq_ref[...] = mm(_P_GQ, W_GLA_QK) * (GLA_DK ** -0.5)
    gk_ref[...] = mm(_P_GK, W_GLA_QK)
    gv_ref[...] = mm(_P_GV, W_D).astype(BF16)
    gates = _dot(mm(_P_GG, LANES).astype(BF16), wgate_ref[...]) + bgate_ref[...]
    la_ref[...] = _log_sigmoid(gates) * (1.0 / GLA_TAU)


def _in_proj(x2, seq, tabs, g, w, qg, wuq, kvg, wukv, wgate, bgate, tm):
    n, d = x2.shape
    nseq = seq // tm
    row = lambda width: pl.BlockSpec((tm, width), lambda i: (i, 0))
    tab = pl.BlockSpec((tm, LANES), lambda i: (i % nseq, 0))
    outs = [(4 * MLA_HEAD_STRIDE, BF16), (4 * MLA_HEAD_STRIDE, BF16), (W_A, BF16),
            (W_DIL_GROUP, BF16), (W_DIL_GROUP, BF16), (W_DIL_GROUP, BF16), (3 * W_C, BF16),
            (W_GLA_QK, F32), (W_GLA_QK, F32), (W_D, BF16), (2 * W_GLA_QK, F32)]
    return pl.pallas_call(
        _in_proj_kernel,
        grid=(n // tm,),
        in_specs=[row(d), tab, tab, tab] + [_const_spec(a.shape) for a in
                                             (g, w, qg, wuq, kvg, wukv, wgate, bgate)],
        out_specs=[row(wd) for wd, _ in outs],
        out_shape=[jax.ShapeDtypeStruct((n, wd), dt) for wd, dt in outs],
        compiler_params=_params("parallel"),
        name="in_proj",
    )(x2, *tabs, g, w, qg, wuq, kvg, wukv, wgate, bgate)


def _mla_kernel(q_ref, k_ref, v_ref, o_ref, vx_ref):
    @pl.when(pl.program_id(2) == 0)
    def _():
        vx_ref[:, :MLA_V] = v_ref[0]
        vx_ref[:, MLA_V:] = jnp.ones((vx_ref.shape[0], MLA_V), BF16)

    c = (MLA_NOPE + MLA_ROPE) ** -0.5 * LOG2_E
    sub = q_ref.shape[1] // MLA_SUB_BLOCKS
    scores = [_dot_nt(q_ref[0, i * sub:(i + 1) * sub, :], k_ref[0]) for i in range(MLA_SUB_BLOCKS)]
    for i, s in enumerate(scores):
        m = jnp.max(s, axis=-1, keepdims=True)
        p = jnp.exp2((s - m) * c).astype(BF16)
        ov = _dot(p, vx_ref[...])
        o_ref[0, i * sub:(i + 1) * sub, :] = (ov[:, :MLA_V] / ov[:, MLA_V:]).astype(o_ref.dtype)


def _mla(q, kc, va, tq):
    b, s, _ = q.shape
    return pl.pallas_call(
        _mla_kernel,
        grid=(b, MLA_HEADS, s // tq),
        in_specs=[pl.BlockSpec((1, tq, MLA_HEAD_STRIDE), lambda i, h, j: (i, j, h)),
                  pl.BlockSpec((1, s, MLA_HEAD_STRIDE), lambda i, h, j: (i, 0, h)),
                  pl.BlockSpec((1, s, MLA_V), lambda i, h, j: (i, 0, h))],
        out_specs=pl.BlockSpec((1, tq, MLA_V), lambda i, h, j: (i, j, h)),
        out_shape=jax.ShapeDtypeStruct((b, s, W_A), BF16),
        scratch_shapes=[pltpu.VMEM((s, 2 * MLA_V), BF16)],
        compiler_params=_params("parallel", "parallel", "arbitrary"),
        name="mla_attention",
    )(q, kc, va)


def _stack_heads(x128):
    lane = lax.broadcasted_iota(jnp.int32, x128.shape, 1)
    zero = jnp.zeros_like(x128)
    return jnp.concatenate([jnp.where(lane < 64, x128, zero), jnp.where(lane >= 64, x128, zero)],
                           axis=0)


def _unstack_heads(y2):
    m = y2.shape[0] // 2
    lane = lax.broadcasted_iota(jnp.int32, (m, y2.shape[1]), 1)
    return jnp.where(lane < 64, y2[:m], y2[m:])


def _pair_scores(q128, k128):
    return _dot_nt(_stack_heads(q128), k128)


def _pair_softmax_pv(s, bias, v128, want_lse):
    s = s + bias
    mx = jnp.max(s, axis=-1, keepdims=True)
    p = jnp.exp(s - mx)
    den = jnp.sum(p, axis=-1, keepdims=True)
    o = _unstack_heads(_dot(p.astype(BF16), v128) / den)
    if not want_lse:
        return o, None
    lse = jnp.broadcast_to(mx + jnp.log(den), (s.shape[0], LANES))
    return o, _unstack_heads(lse)


def _dil_kernel(qkv_ref, bias_ref, o_ref, lse_ref, s_ref, *, length, qb, win, radius, group):
    def body(i, carry):
        jobs = []
        for u in range(group):
            q0 = pl.multiple_of((i * group + u) * qb, qb)
            ks = pl.multiple_of(jnp.clip(q0 - radius, 0, length - win), 64)
            for c in range(0, W_B, LANES):
                q128 = qkv_ref[0, pl.ds(q0, qb), c:c + LANES]
                k128 = qkv_ref[0, pl.ds(ks, win), W_B + c:W_B + c + LANES]
                s_ref[len(jobs)] = _pair_scores(q128, k128)
                jobs.append((q0, ks, c))
        for slot, (q0, ks, c) in enumerate(jobs):
            v128 = qkv_ref[0, pl.ds(ks, win), 2 * W_B + c:2 * W_B + c + LANES]
            o, lse = _pair_softmax_pv(s_ref[slot], bias_ref[(q0 - ks) // 64], v128, True)
            o_ref[0, pl.ds(q0, qb), c:c + LANES] = o.astype(o_ref.dtype)
            lse_ref[0, pl.ds(q0, qb), c:c + LANES] = lse
        return carry

    lax.fori_loop(0, length // (qb * group), body, 0)


def _dil_bias_table(qb, win, radius):
    nvar = (win - qb) // 64 + 1
    i = (np.arange(2 * qb) % qb)[None, :, None]
    j = np.arange(win)[None, None, :]
    delta = (64 * np.arange(nvar))[:, None, None]
    return jnp.asarray(np.where(np.abs(i + delta - j) <= radius, 0.0, NEG_INF), F32)


def _dilated_group(qkv, window, dilation):
    b, s, _ = qkv.shape
    r = dilation
    radius = window // (2 * dilation)
    length = s // r
    qb = min(128, length)
    win = min(qb + 2 * radius, length)
    assert length % qb == 0 and radius % 64 == 0 and qb % 64 == 0
    table = _dil_bias_table(qb, win, radius)
    group = 2 if (length // qb) % 2 == 0 else 1
    view = qkv.reshape(b, length, r * W_DIL_GROUP)
    o, lse = pl.pallas_call(
        functools.partial(_dil_kernel, length=length, qb=qb, win=win, radius=radius, group=group),
        grid=(b, r),
        in_specs=[pl.BlockSpec((1, length, W_DIL_GROUP), lambda i, m: (i, 0, m)),
                  _const_spec(table.shape)],
        out_specs=[pl.BlockSpec((1, length, W_B), lambda i, m: (i, 0, m)),
                   pl.BlockSpec((1, length, W_B), lambda i, m: (i, 0, m))],
        out_shape=[jax.ShapeDtypeStruct((b, length, r * W_B), BF16),
                   jax.ShapeDtypeStruct((b, length, r * W_B), F32)],
        scratch_shapes=[pltpu.VMEM((group * DIL_HEADS // 2, 2 * qb, win), F32)],
        compiler_params=_params("parallel", "parallel"),
        name=f"dilated_attention_r{r}",
    )(view, table)
    return o.reshape(b, s, W_B), lse.reshape(b, s, W_B)


def _nat_bias_table(rpb, kh):
    qc = np.arange(GRID_W)[:, None]
    kc = np.arange(GRID_W)[None, :]
    win_start = np.clip(qc - NAT_KW // 2, 0, GRID_W - NAT_KW)
    allowed = (kc >= win_start) & (kc < win_start + NAT_KW)
    col_off = np.clip(kc - qc + NAT_KW - 1, 0, 2 * NAT_KW - 2)
    n_var = NAT_KH_MAX
    row_off = np.clip(np.arange(n_var)[:, None] + np.arange(kh)[None, :], 0, 2 * NAT_KH_MAX - 2)
    t = rpb[:, row_off]
    t = t[:, :, :, col_off]
    t = jnp.where(allowed[None, None, None], t, NEG_INF)
    t = t.transpose(0, 1, 3, 2, 4)
    t = t.reshape(NAT_HEADS // 2, 2, n_var, GRID_W, kh * GRID_W).transpose(0, 2, 1, 3, 4)
    return t.reshape(NAT_HEADS // 2, n_var, 2 * GRID_W, kh * GRID_W).astype(F32)


def _nat_kernel(q_ref, k_ref, v_ref, bias_ref, o_ref, s_ref, *, rows, kh, group):
    def body(i, carry):
        jobs = []
        for u in range(group):
            r = i * group + u
            rs = jnp.clip(r - kh // 2, 0, rows - kh)
            q0 = pl.multiple_of(r * GRID_W, GRID_W)
            k0 = pl.multiple_of(rs * GRID_W, GRID_W)
            s_ref[u] = _pair_scores(q_ref[0, pl.ds(q0, GRID_W), :],
                                    k_ref[0, pl.ds(k0, kh * GRID_W), :])
            jobs.append((q0, k0, rs - r + NAT_KH_MAX - 1))
        for u, (q0, k0, var) in enumerate(jobs):
            v128 = v_ref[0, pl.ds(k0, kh * GRID_W), :]
            o, _ = _pair_softmax_pv(s_ref[u], bias_ref[0, var], v128, False)
            o_ref[0, pl.ds(q0, GRID_W), :] = o.astype(o_ref.dtype)
        return carry

    lax.fori_loop(0, rows // group, body, 0)


def _nat(qkv, rpb):
    b, s, _ = qkv.shape
    rows = s // GRID_W
    kh = min(NAT_KH_MAX, rows)
    table = _nat_bias_table(rpb, kh)
    npair = NAT_HEADS // 2
    col = lambda off: pl.BlockSpec((1, s, LANES), lambda i, p: (i, 0, off + p))
    group = 8 if rows % 8 == 0 else 1
    return pl.pallas_call(
        functools.partial(_nat_kernel, rows=rows, kh=kh, group=group),
        grid=(b, npair),
        in_specs=[col(0), col(npair), col(2 * npair),
                  pl.BlockSpec((1,) + table.shape[1:], lambda i, p: (p, 0, 0, 0))],
        out_specs=pl.BlockSpec((1, s, LANES), lambda i, p: (i, 0, p)),
        out_shape=jax.ShapeDtypeStruct((b, s, W_C), BF16),
        scratch_shapes=[pltpu.VMEM((group, 2 * GRID_W, kh * GRID_W), F32)],
        compiler_params=_params("parallel", "parallel"),
        name="neighbourhood_attention",
    )(qkv, qkv, qkv, table)


def _split3(x):
    hi = x.astype(BF16)
    r1 = x - hi.astype(F32)
    mid = r1.astype(BF16)
    lo = (r1 - mid.astype(F32)).astype(BF16)
    return hi, mid, lo


def _gla_kernel(q_ref, k_ref, v_ref, laf_ref, lab_ref, g_ref, o_ref, accf_ref, accb_ref, st_ref,
                *, seq):
    c = GLA_CHUNK
    nchunk = seq // c
    ri = lax.broadcasted_iota(jnp.int32, (c, c), 0)
    ci = lax.broadcasted_iota(jnp.int32, (c, c), 1)
    r2 = lax.broadcasted_iota(jnp.int32, (2 * c, 2 * c), 0)
    c2 = lax.broadcasted_iota(jnp.int32, (2 * c, 2 * c), 1)
    same_head = (r2 >= c) == (c2 >= c)
    zero_v = jnp.zeros((c, GLA_DV), BF16)
    st_ref[...] = jnp.zeros_like(st_ref)

    dirs = ((laf_ref, accf_ref, ci <= ri, same_head & (c2 <= r2), c - 1, c // 2 - 1),
            (lab_ref, accb_ref, ci >= ri, same_head & (c2 >= r2), 0, c // 2))
    group = 2 if nchunk % 2 == 0 else 1

    def body(i, carry):
        jobs = []
        for d, (la_ref, acc_ref, tri, keep, total_row, mid_row) in enumerate(dirs):
            for u in range(group):
                n = i * group + u
                r0 = pl.multiple_of((nchunk - 1 - n if d else n) * c, c)
                hi, mid, lo = _split3(la_ref[0, pl.ds(r0, c), :])
                tri_b = tri.astype(BF16)
                bcum = _dot(tri_b, hi) + _dot(tri_b, mid) + _dot(tri_b, lo)
                jobs.append(dict(d=d, r0=r0, bcum=bcum, keep=keep, acc=acc_ref,
                                 b_tot=bcum[total_row:total_row + 1, :],
                                 b_mid=bcum[mid_row:mid_row + 1, :]))
        for j in jobs:
            q = q_ref[0, pl.ds(j["r0"], c), :]
            k = k_ref[0, pl.ds(j["r0"], c), :]
            v = v_ref[0, pl.ds(j["r0"], c), :]
            q_in = _stack_heads(q * jnp.exp(j["bcum"] - j["b_mid"])).astype(BF16)
            k_in = _stack_heads(k * jnp.exp(j["b_mid"] - j["bcum"])).astype(BF16)
            k_dec = (k * jnp.exp(j["b_tot"] - j["bcum"])).astype(BF16)
            j["q_st"] = _stack_heads(q * jnp.exp(j["bcum"])).astype(BF16)
            j["kv_t"] = _dot(v.astype(F32).T.astype(BF16), k_dec)
            j["att"] = _dot_nt(q_in, k_in)
            j["v2"] = jnp.concatenate([jnp.concatenate([v[:, :GLA_DV], zero_v], axis=1),
                                       jnp.concatenate([zero_v, v[:, GLA_DV:]], axis=1)], axis=0)
        states = [st_ref[0], st_ref[1]]
        for j in jobs:
            state = states[j["d"]]
            j["inter"] = _dot_nt(j["q_st"], state.astype(BF16))
            states[j["d"]] = state * jnp.exp(j["b_tot"]) + j["kv_t"]
        st_ref[0], st_ref[1] = states
        for j in jobs:
            att = jnp.where(j["keep"], j["att"], 0.0).astype(BF16)
            tot = _dot(att, j["v2"]) + j["inter"]
            j["acc"][pl.ds(j["r0"], c), :] = jnp.concatenate(
                [tot[:c, :GLA_DV], tot[c:, GLA_DV:]], axis=1)
        return carry

    lax.fori_loop(0, nchunk // group, body, 0)
    for hh in range(2):
        sl = slice(hh * GLA_DV, (hh + 1) * GLA_DV)
        o_ref[0, :, sl] = _rms(accf_ref[:, sl] + accb_ref[:, sl], g_ref[:, sl]).astype(o_ref.dtype)


def _gla(gq, gk, gv, la, gain):
    b, s, _ = gq.shape
    npair = GLA_HEADS // 2
    qk = pl.BlockSpec((1, s, LANES), lambda i, p: (i, 0, p))
    return pl.pallas_call(
        functools.partial(_gla_kernel, seq=s),
        grid=(b, npair),
        in_specs=[qk, qk,
                  pl.BlockSpec((1, s, 2 * GLA_DV), lambda i, p: (i, 0, p)),
                  pl.BlockSpec((1, s, LANES), lambda i, p: (i, 0, p)),
                  pl.BlockSpec((1, s, LANES), lambda i, p: (i, 0, npair + p)),
                  pl.BlockSpec((1, 2 * GLA_DV), lambda i, p: (0, p))],
        out_specs=pl.BlockSpec((1, s, 2 * GLA_DV), lambda i, p: (i, 0, p)),
        out_shape=jax.ShapeDtypeStruct((b, s, W_D), BF16),
        scratch_shapes=[pltpu.VMEM((s, 2 * GLA_DV), F32), pltpu.VMEM((s, 2 * GLA_DV), F32),
                        pltpu.VMEM((2, 2 * GLA_DV, LANES), F32)],
        compiler_params=_params("parallel", "parallel"),
        name="gated_linear_attention",
    )(gq, gk, gv, la, la, gain)


def _merge_kernel(x_ref, g_ref, oa_ref, ob0_ref, ob1_ref, ob2_ref, l0_ref, l1_ref, l2_ref,
                  oc_ref, od_ref, wz_ref, wm_ref, bm_ref, wp_ref, wo_ref, fg_ref, y_ref,
                  *, final):
    x = x_ref[...]
    d = x.shape[-1]
    hb = _rms(x, g_ref[...]).astype(BF16)

    lses = (l0_ref[...], l1_ref[...], l2_ref[...])
    mx = jnp.maximum(jnp.maximum(lses[0], lses[1]), lses[2])
    es = [jnp.exp(l - mx) for l in lses]
    tot = es[0] + es[1] + es[2]
    o_b = sum(e / tot * o[...].astype(F32) for e, o in zip(es, (ob0_ref, ob1_ref, ob2_ref)))

    branches = ((oa_ref[...].astype(F32), W_A), (o_b, W_B), (oc_ref[...].astype(F32), W_C),
                (od_ref[...].astype(F32), W_D))
    mixed = jnp.zeros(x.shape, F32)
    off = 0
    for i, (o, width) in enumerate(branches):
        z = _dot(hb, wz_ref[:, off:off + width])
        u = (o * (z * _sigmoid(z))).astype(BF16)
        proj = _dot(u, wp_ref[off:off + width, :])
        gate = _sigmoid(_dot(hb, wm_ref[:, i * d:(i + 1) * d]) + bm_ref[:, i * d:(i + 1) * d])
        mixed = mixed + gate * proj
        off += width
    y = x + _dot(mixed.astype(BF16), wo_ref[...])
    if final:
        y = _rms(y, fg_ref[...])
    y_ref[...] = y


def _merge(x2, g, oa, obs, lses, oc, od, wz, wm, bm, wp, wo, fg, final, tm):
    n, d = x2.shape
    row = lambda width: pl.BlockSpec((tm, width), lambda i: (i, 0))
    consts = (wz, wm, bm, wp, wo, fg)
    return pl.pallas_call(
        functools.partial(_merge_kernel, final=final),
        grid=(n // tm,),
        in_specs=[row(d), _const_spec(g.shape), row(W_A)] + [row(W_B)] * 6 + [row(W_C), row(W_D)]
                 + [_const_spec(a.shape) for a in consts],
        out_specs=row(d),
        out_shape=jax.ShapeDtypeStruct((n, d), F32),
        compiler_params=_params("parallel"),
        name="gate_merge_out",
    )(x2, g, oa, *obs, *lses, oc, od, *consts)


def _rope_tables(seq):
    half = MLA_ROPE // 2
    inv = jnp.power(ROPE_THETA, -jnp.arange(0, MLA_ROPE, 2, dtype=F32) / MLA_ROPE)
    ang = jnp.arange(seq, dtype=jnp.int32).astype(F32)[:, None] * inv[None, :]
    cos, sin = jnp.cos(ang), jnp.sin(ang)
    zero = jnp.zeros((seq, half), F32)
    cos_t = jnp.concatenate([cos] * 4, axis=-1)
    sin_hi = jnp.concatenate([zero, sin, zero, sin], axis=-1)
    sin_lo = jnp.concatenate([-sin, zero, -sin, zero], axis=-1)
    return cos_t, sin_hi, sin_lo


def _prep_layer(w_in, w_uq, w_ukv, w_gf, b_gf, w_gb, b_gb):
    d = w_in.shape[0]
    o = _IN_OFF
    zpad = lambda width: jnp.zeros((d, width), w_in.dtype)
    w_pad = jnp.concatenate([
        w_in[:, o[0]:o[3]], zpad(LANES - MLA_ROPE),
        w_in[:, o[3]:o[10]],
        w_in[:, o[10]:o[12]], zpad(LANES - 2 * GLA_GATE_RANK)], axis=1).astype(BF16)
    w_z = w_in[:, o[12]:o[16]].astype(BF16)
    uq = w_uq.reshape(MLA_Q_RANK, MLA_HEADS, MLA_NOPE + MLA_ROPE)
    uq = jnp.pad(uq, ((0, 0), (0, 0), (0, MLA_HEAD_STRIDE - MLA_NOPE - MLA_ROPE)))
    uq = uq.reshape(MLA_Q_RANK, MLA_HEADS * MLA_HEAD_STRIDE).astype(BF16)
    ukv = w_ukv.reshape(MLA_KV_RANK, MLA_HEADS, MLA_NOPE + MLA_V)
    ukv = jnp.concatenate([ukv[:, :, :MLA_NOPE].reshape(MLA_KV_RANK, -1),
                           ukv[:, :, MLA_NOPE:].reshape(MLA_KV_RANK, -1)], axis=1).astype(BF16)
    wgate = jnp.zeros((LANES, 2 * W_GLA_QK), F32)
    wgate = wgate.at[:GLA_GATE_RANK, :W_GLA_QK].set(w_gf)
    wgate = wgate.at[GLA_GATE_RANK:2 * GLA_GATE_RANK, W_GLA_QK:].set(w_gb)
    bgate = jnp.concatenate([b_gf, b_gb])[None, :]
    return w_pad, w_z, uq, ukv, wgate.astype(BF16), bgate


def kernel(x, norm_g, w_in, mla_q_norm_g, mla_w_uq, mla_kv_norm_g, mla_w_ukv, nat_rpb,
           gla_w_gate_f, gla_b_gate_f, gla_w_gate_b, gla_b_gate_b, gla_norm_g,
           w_proj_a, w_proj_b, w_proj_c, w_proj_d, w_merge, b_merge, w_out, final_norm_g):
    b, s, d = x.shape
    depth = w_in.shape[0]
    tm = min(512, s)
    tq = min(512, s)
    tabs = _rope_tables(s)
    x2 = x.reshape(b * s, d)
    for l in range(depth):
        w_pad, w_z, uq, ukv, wgate, bgate = _prep_layer(
            w_in[l], mla_w_uq[l], mla_w_ukv[l], gla_w_gate_f[l], gla_b_gate_f[l],
            gla_w_gate_b[l], gla_b_gate_b[l])
        (q, kc, va, d0, d1, d2, nat_qkv, gq, gk, gv, la) = _in_proj(
            x2, s, tabs, norm_g[l][None, :], w_pad, mla_q_norm_g[l][None, :], uq,
            mla_kv_norm_g[l][None, :], ukv, wgate, bgate, tm)
        sh = lambda t: t.reshape(b, s, t.shape[-1])
        o_a = _mla(sh(q), sh(kc), sh(va), tq)
        obs, lses = [], []
        for dil_qkv, (window, dilation) in zip((d0, d1, d2), DIL_PATTERNS):
            o_g, lse_g = _dilated_group(sh(dil_qkv), window, dilation)
            obs.append(o_g.reshape(b * s, W_B))
            lses.append(lse_g.reshape(b * s, W_B))
        o_c = _nat(sh(nat_qkv), nat_rpb[l])
        o_d = _gla(sh(gq), sh(gk), sh(gv), sh(la), gla_norm_g[l].reshape(1, W_D))
        w_p = jnp.concatenate([w_proj_a[l], w_proj_b[l], w_proj_c[l], w_proj_d[l]], axis=0)
        x2 = _merge(x2, norm_g[l][None, :], o_a.reshape(b * s, W_A), obs, lses,
                    o_c.reshape(b * s, W_C), o_d.reshape(b * s, W_D), w_z,
                    w_merge[l].astype(BF16), b_merge[l][None, :], w_p.astype(BF16),
                    w_out[l].astype(BF16), final_norm_g[None, :], l == depth - 1, tm)
    return x2.reshape(b, s, d)
```

```python
import functools

import numpy as np
import jax
import jax.numpy as jnp
from jax import lax
from jax.experimental import pallas as pl
from jax.experimental.pallas import tpu as pltpu

F32 = jnp.float32
BF16 = jnp.bfloat16

GRID_W = 64
ROPE_THETA = 10000.0
NORM_EPS = 1e-6
NEG_INF = -1e30

MLA_HEADS = 4
MLA_NOPE = 128
MLA_ROPE = 64
MLA_V = 128
MLA_Q_RANK = 256
MLA_KV_RANK = 128
DIL_PATTERNS = ((128, 1), (512, 4), (2048, 16))
DIL_HEADS = 4
DIL_DH = 64
NAT_HEADS = 8
NAT_DH = 64
NAT_KH_MAX = 8
NAT_KW = 16
GLA_HEADS = 4
GLA_DK = 64
GLA_DV = 128
GLA_GATE_RANK = 16
GLA_TAU = 16.0
GLA_CHUNK = 64

W_A = MLA_HEADS * MLA_V
W_B = DIL_HEADS * DIL_DH
W_C = NAT_HEADS * NAT_DH
W_D = GLA_HEADS * GLA_DV
N_BRANCH = 4
W_DIL_GROUP = 3 * W_B
W_GLA_QK = GLA_HEADS * GLA_DK

LOG2_E = 1.4426950408889634
LANES = 128
MLA_HEAD_STRIDE = 256
MLA_SUB_QUERIES = 256
VMEM_LIMIT = 56 * 1024 * 1024

_IN_SIZES = (MLA_Q_RANK, MLA_KV_RANK, MLA_ROPE, 3 * W_DIL_GROUP, W_C, W_C, W_C,
             W_GLA_QK, W_GLA_QK, W_D, GLA_GATE_RANK, GLA_GATE_RANK, W_A, W_B, W_C, W_D)
_IN_OFF = tuple(int(v) for v in np.concatenate([[0], np.cumsum(_IN_SIZES)]))

_P_Q = 0
_P_KV = _P_Q + MLA_Q_RANK
_P_KR = _P_KV + MLA_KV_RANK
_P_DIL = _P_KR + LANES
_P_NAT = _P_DIL + 3 * W_DIL_GROUP
_P_GQ = _P_NAT + 3 * W_C
_P_GK = _P_GQ + W_GLA_QK
_P_GV = _P_GK + W_GLA_QK
_P_GG = _P_GV + W_D
_P_END = _P_GG + LANES


def _dot(a, b):
    return jnp.dot(a, b, preferred_element_type=F32)


def _dot_nt(a, b):
    return lax.dot_general(a, b, (((1,), (1,)), ((), ())), preferred_element_type=F32)


def _rms(x, g):
    return x * lax.rsqrt(jnp.mean(x * x, axis=-1, keepdims=True) + NORM_EPS) * g


def _rope128(x, cos, sin_hi, sin_lo):
    return x * cos + pltpu.roll(x, 32, 1) * sin_hi + pltpu.roll(x, LANES - 32, 1) * sin_lo


def _sigmoid(x):
    return 1.0 / (1.0 + jnp.exp(-x))


def _log_sigmoid(x):
    return jnp.minimum(x, 0.0) - jnp.log1p(jnp.exp(-jnp.abs(x)))


def _params(*sem):
    return pltpu.CompilerParams(dimension_semantics=sem, vmem_limit_bytes=VMEM_LIMIT)


def _const_spec(shape):
    nd = len(shape)
    return pl.BlockSpec(shape, lambda *_: (0,) * nd, pipeline_mode=pl.Buffered(1))


_PERM_DILATIONS = tuple(r for _, r in DIL_PATTERNS if r > 1)


def _perm_slot(r):
    return _PERM_DILATIONS.index(r)


def _perm_tables(tm, inverse):
    tabs = []
    for r in _PERM_DILATIONS:
        p = np.zeros((tm, tm), np.float32)
        tok = np.arange(tm)
        p[(tok % r) * (tm // r) + tok // r, tok] = 1.0
        tabs.append(p.T if inverse else p)
    return jnp.asarray(np.stack(tabs), BF16)


def _in_proj_kernel(x_ref, cos_ref, shi_ref, slo_ref, g_ref, w_ref, qg_ref, wuq_ref, kvg_ref,
                    wukv_ref, wgate_ref, bgate_ref, perm_ref,
                    q_ref, kc_ref, va_ref, d0_ref, d1_ref, d2_ref, nat_ref, gq_ref, gk_ref,
                    gv_ref, la_ref, y_ref):
    tm = x_ref.shape[0]
    hb = _rms(x_ref[...], g_ref[...]).astype(BF16)
    cos, shi, slo = cos_ref[...], shi_ref[...], slo_ref[...]

    def mm(a, width):
        return _dot(hb, w_ref[:, a:a + width])

    def rope(t):
        return _rope128(t, cos, shi, slo)

    qn = _rms(mm(_P_Q, MLA_Q_RANK), qg_ref[...]).astype(BF16)
    for h in range(MLA_HEADS):
        base = h * MLA_HEAD_STRIDE
        qh = _dot(qn, wuq_ref[:, base:base + MLA_HEAD_STRIDE])
        q_ref[:, base:base + LANES] = qh[:, :LANES].astype(BF16)
        q_ref[:, base + LANES:base + 2 * LANES] = rope(qh[:, LANES:]).astype(BF16)
    kv_kr = mm(_P_KV, MLA_KV_RANK + LANES)
    kvn = _rms(kv_kr[:, :MLA_KV_RANK], kvg_ref[...]).astype(BF16)
    kpe = rope(kv_kr[:, MLA_KV_RANK:]).astype(BF16)
    k_nope = _dot(kvn, wukv_ref[:, :W_A])
    for h in range(MLA_HEADS):
        base = h * MLA_HEAD_STRIDE
        kc_ref[:, base:base + LANES] = k_nope[:, h * LANES:(h + 1) * LANES].astype(BF16)
        kc_ref[:, base + LANES:base + 2 * LANES] = kpe
    va_ref[...] = _dot(kvn, wukv_ref[:, W_A:2 * W_A]).astype(BF16)

    for g, (d_ref, (_, r)) in enumerate(zip((d0_ref, d1_ref, d2_ref), DIL_PATTERNS)):
        base = _P_DIL + g * W_DIL_GROUP
        qk = mm(base, 2 * W_B)
        for c in range(0, 2 * W_B, LANES):
            fold = DIL_DH ** -0.5 if c < W_B else 1.0
            y_ref[g, :, c:c + LANES] = (rope(qk[:, c:c + LANES]) * fold).astype(BF16)
        y_ref[g, :, 2 * W_B:] = mm(base + 2 * W_B, W_B).astype(BF16)
        y = y_ref[g]
        if r > 1:
            y = _dot(perm_ref[_perm_slot(r)], y).astype(BF16)
        d_ref[0] = y.reshape(r, tm // r, W_DIL_GROUP)

    nat_ref[:, :W_C] = (mm(_P_NAT, W_C) * (NAT_DH ** -0.5)).astype(BF16)
    for c in range(W_C, 3 * W_C, W_C):
        nat_ref[:, c:c + W_C] = mm(_P_NAT + c, W_C).astype(BF16)

    gq_ref[...] = mm(_P_GQ, W_GLA_QK) * (GLA_DK ** -0.5)
    gk_ref[...] = mm(_P_GK, W_GLA_QK)
    gv_ref[...] = mm(_P_GV, W_D).astype(BF16)
    gates = _dot(mm(_P_GG, LANES).astype(BF16), wgate_ref[...]) + bgate_ref[...]
    la_ref[...] = _log_sigmoid(gates) * (1.0 / GLA_TAU)


def _in_proj(x2, seq, tabs, g, w, qg, wuq, kvg, wukv, wgate, bgate, tm):
    n, d = x2.shape
    nseq = seq // tm
    batch = n // seq
    row = lambda width: pl.BlockSpec((tm, width), lambda i: (i, 0))
    tab = pl.BlockSpec((tm, LANES), lambda i: (i % nseq, 0))
    sub = lambda r, width: pl.BlockSpec((1, r, tm // r, width),
                                        lambda i: (i // nseq, 0, i % nseq, 0))
    sub_shape = lambda r, width, dt: jax.ShapeDtypeStruct((batch, r, seq // r, width), dt)
    dil = [r for _, r in DIL_PATTERNS]
    outs = [(4 * MLA_HEAD_STRIDE, BF16), (4 * MLA_HEAD_STRIDE, BF16), (W_A, BF16)]
    outs2 = [(3 * W_C, BF16), (W_GLA_QK, F32), (W_GLA_QK, F32), (W_D, BF16), (2 * W_GLA_QK, F32)]
    perm = _perm_tables(tm, False)
    consts = (g, w, qg, wuq, kvg, wukv, wgate, bgate, perm)
    return pl.pallas_call(
        _in_proj_kernel,
        grid=(n // tm,),
        in_specs=[row(d), tab, tab, tab] + [_const_spec(a.shape) for a in consts],
        out_specs=([row(wd) for wd, _ in outs] + [sub(r, W_DIL_GROUP) for r in dil]
                   + [row(wd) for wd, _ in outs2]),
        out_shape=([jax.ShapeDtypeStruct((n, wd), dt) for wd, dt in outs]
                   + [sub_shape(r, W_DIL_GROUP, BF16) for r in dil]
                   + [jax.ShapeDtypeStruct((n, wd), dt) for wd, dt in outs2]),
        scratch_shapes=[pltpu.VMEM((len(dil), tm, W_DIL_GROUP), BF16)],
        compiler_params=_params("parallel"),
        name="in_proj",
    )(x2, *tabs, *consts)


def _mla_kernel(q_ref, k_ref, v_ref, o_ref, vx_ref):
    @pl.when(pl.program_id(2) == 0)
    def _():
        vx_ref[:, :MLA_V] = v_ref[0]
        vx_ref[:, MLA_V:] = jnp.ones((vx_ref.shape[0], MLA_V), BF16)

    c = (MLA_NOPE + MLA_ROPE) ** -0.5 * LOG2_E
    sub = MLA_SUB_QUERIES
    nsub = q_ref.shape[1] // sub
    scores = [_dot_nt(q_ref[0, i * sub:(i + 1) * sub, :], k_ref[0]) for i in range(nsub)]
    for i, s in enumerate(scores):
        m = jnp.max(s, axis=-1, keepdims=True)
        p = jnp.exp2((s - m) * c).astype(BF16)
        ov = _dot(p, vx_ref[...])
        o_ref[0, i * sub:(i + 1) * sub, :] = (ov[:, :MLA_V] / ov[:, MLA_V:]).astype(o_ref.dtype)


def _mla(q, kc, va, tq):
    b, s, _ = q.shape
    return pl.pallas_call(
        _mla_kernel,
        grid=(b, MLA_HEADS, s // tq),
        in_specs=[pl.BlockSpec((1, tq, MLA_HEAD_STRIDE), lambda i, h, j: (i, j, h)),
                  pl.BlockSpec((1, s, MLA_HEAD_STRIDE), lambda i, h, j: (i, 0, h)),
                  pl.BlockSpec((1, s, MLA_V), lambda i, h, j: (i, 0, h))],
        out_specs=pl.BlockSpec((1, tq, MLA_V), lambda i, h, j: (i, j, h)),
        out_shape=jax.ShapeDtypeStruct((b, s, W_A), BF16),
        scratch_shapes=[pltpu.VMEM((s, 2 * MLA_V), BF16)],
        compiler_params=_params("parallel", "parallel", "arbitrary"),
        name="mla_attention",
    )(q, kc, va)


def _stack_heads(x128):
    lane = lax.broadcasted_iota(jnp.int32, x128.shape, 1)
    zero = jnp.zeros_like(x128)
    return jnp.concatenate([jnp.where(lane < 64, x128, zero), jnp.where(lane >= 64, x128, zero)],
                           axis=0)


def _unstack_heads(y2):
    m = y2.shape[0] // 2
    lane = lax.broadcasted_iota(jnp.int32, (m, y2.shape[1]), 1)
    return jnp.where(lane < 64, y2[:m], y2[m:])


def _pair_scores(q128, k128):
    return _dot_nt(_stack_heads(q128), k128)


def _pair_softmax_pv(s, bias, v128, want_lse):
    s = s + bias
    mx = jnp.max(s, axis=-1, keepdims=True)
    p = jnp.exp(s - mx)
    den = jnp.sum(p, axis=-1, keepdims=True)
    o = _unstack_heads(_dot(p.astype(BF16), v128) / den)
    if not want_lse:
        return o, None
    lse = jnp.broadcast_to(mx + jnp.log(den), (s.shape[0], LANES))
    return o, _unstack_heads(lse)


def _dil_kernel(qkv_ref, bias_ref, o_ref, lse_ref, s_ref, *, length, qb, win, radius, group):
    def body(i, carry):
        jobs = []
        for u in range(group):
            q0 = pl.multiple_of((i * group + u) * qb, qb)
            ks = pl.multiple_of(jnp.clip(q0 - radius, 0, length - win), 64)
            for c in range(0, W_B, LANES):
                q128 = qkv_ref[pl.ds(q0, qb), c:c + LANES]
                k128 = qkv_ref[pl.ds(ks, win), W_B + c:W_B + c + LANES]
                s_ref[len(jobs)] = _pair_scores(q128, k128)
                jobs.append((q0, ks, c))
        for slot, (q0, ks, c) in enumerate(jobs):
            v128 = qkv_ref[pl.ds(ks, win), 2 * W_B + c:2 * W_B + c + LANES]
            o, lse = _pair_softmax_pv(s_ref[slot], bias_ref[(q0 - ks) // 64], v128, True)
            o_ref[pl.ds(q0, qb), c:c + LANES] = o.astype(o_ref.dtype)
            lse_ref[pl.ds(q0, qb), c:c + LANES] = lse
        return carry

    lax.fori_loop(0, length // (qb * group), body, 0)


def _dil_bias_table(qb, win, radius):
    nvar = (win - qb) // 64 + 1
    i = (np.arange(2 * qb) % qb)[None, :, None]
    j = np.arange(win)[None, None, :]
    delta = (64 * np.arange(nvar))[:, None, None]
    return jnp.asarray(np.where(np.abs(i + delta - j) <= radius, 0.0, NEG_INF), F32)


def _dilated_group(qkv, window):
    b, r, length, _ = qkv.shape
    radius = window // (2 * r)
    qb = min(128, length)
    win = min(qb + 2 * radius, length)
    assert length % qb == 0 and radius % 64 == 0 and qb % 64 == 0
    table = _dil_bias_table(qb, win, radius)
    group = 2 if (length // qb) % 2 == 0 else 1
    sub = lambda width: pl.BlockSpec((None, None, length, width), lambda i, m: (i, m, 0, 0))
    return pl.pallas_call(
        functools.partial(_dil_kernel, length=length, qb=qb, win=win, radius=radius, group=group),
        grid=(b, r),
        in_specs=[sub(W_DIL_GROUP), _const_spec(table.shape)],
        out_specs=[sub(W_B), sub(W_B)],
        out_shape=[jax.ShapeDtypeStruct((b, r, length, W_B), BF16),
                   jax.ShapeDtypeStruct((b, r, length, W_B), F32)],
        scratch_shapes=[pltpu.VMEM((group * DIL_HEADS // 2, 2 * qb, win), F32)],
        compiler_params=_params("parallel", "parallel"),
        name=f"dilated_attention_r{r}",
    )(qkv, table)


def _nat_bias_table(rpb):
    qc = np.arange(GRID_W)[:, None]
    kc = np.arange(GRID_W)[None, :]
    win_start = np.clip(qc - NAT_KW // 2, 0, GRID_W - NAT_KW)
    allowed = (kc >= win_start) & (kc < win_start + NAT_KW)
    n_row = 2 * NAT_KH_MAX - 1
    side = GRID_W - NAT_KW
    g = jnp.pad(rpb, ((0, 0), (0, 0), (side, side + 1)))
    a = jnp.broadcast_to(g[:, :, None, :], (NAT_HEADS, n_row, GRID_W, 2 * GRID_W))
    a = a.reshape(NAT_HEADS, n_row, 2 * GRID_W * GRID_W)
    a = a[:, :, GRID_W - 1:GRID_W - 1 + GRID_W * (2 * GRID_W - 1)]
    toep = a.reshape(NAT_HEADS, n_row, GRID_W, 2 * GRID_W - 1)[..., :GRID_W]
    toep = jnp.where(allowed[None, None], toep, NEG_INF).astype(F32)
    return jnp.concatenate([toep[:, :-1], toep[:, 1:]], axis=-1)


def _nat_kernel(q_ref, k_ref, v_ref, bias_ref, o_ref, s_ref, *, rows, group):
    kh = NAT_KH_MAX

    def body(i, carry):
        jobs = []
        for u in range(group):
            r = i * group + u
            rs = jnp.clip(r - kh // 2, 0, rows - kh)
            q0 = pl.multiple_of(r * GRID_W, GRID_W)
            k0 = pl.multiple_of(rs * GRID_W, GRID_W)
            s_ref[u] = _pair_scores(q_ref[0, pl.ds(q0, GRID_W), :],
                                    k_ref[0, pl.ds(k0, kh * GRID_W), :])
            jobs.append((q0, k0, rs - r + NAT_KH_MAX - 1))
        for u, (q0, k0, var) in enumerate(jobs):
            v128 = v_ref[0, pl.ds(k0, kh * GRID_W), :]
            bias = jnp.concatenate(
                [jnp.concatenate([bias_ref[hh, var + j] for j in range(0, kh, 2)], axis=1)
                 for hh in range(2)], axis=0)
            o, _ = _pair_softmax_pv(s_ref[u], bias, v128, False)
            o_ref[0, pl.ds(q0, GRID_W), :] = o.astype(o_ref.dtype)
        return carry

    lax.fori_loop(0, rows // group, body, 0)


def _nat(qkv, rpb):
    b, s, _ = qkv.shape
    rows = s // GRID_W
    assert rows >= NAT_KH_MAX and NAT_KH_MAX % 2 == 0
    table = _nat_bias_table(rpb)
    npair = NAT_HEADS // 2
    col = lambda off: pl.BlockSpec((1, s, LANES), lambda i, p: (i, 0, off + p))
    group = 8 if rows % 8 == 0 else 1
    return pl.pallas_call(
        functools.partial(_nat_kernel, rows=rows, group=group),
        grid=(b, npair),
        in_specs=[col(0), col(npair), col(2 * npair),
                  pl.BlockSpec((2,) + table.shape[1:], lambda i, p: (p, 0, 0, 0))],
        out_specs=pl.BlockSpec((1, s, LANES), lambda i, p: (i, 0, p)),
        out_shape=jax.ShapeDtypeStruct((b, s, W_C), BF16),
        scratch_shapes=[pltpu.VMEM((group, 2 * GRID_W, NAT_KH_MAX * GRID_W), F32)],
        compiler_params=_params("parallel", "parallel"),
        name="neighbourhood_attention",
    )(qkv, qkv, qkv, table)


def _split3(x):
    hi = x.astype(BF16)
    r1 = x - hi.astype(F32)
    mid = r1.astype(BF16)
    lo = (r1 - mid.astype(F32)).astype(BF16)
    return hi, mid, lo


def _gla_kernel(q_ref, k_ref, v_ref, laf_ref, lab_ref, g_ref, o_ref, accf_ref, accb_ref, st_ref,
                *, seq):
    c = GLA_CHUNK
    nchunk = seq // c
    ri = lax.broadcasted_iota(jnp.int32, (c, c), 0)
    ci = lax.broadcasted_iota(jnp.int32, (c, c), 1)
    r2 = lax.broadcasted_iota(jnp.int32, (2 * c, 2 * c), 0)
    c2 = lax.broadcasted_iota(jnp.int32, (2 * c, 2 * c), 1)
    same_head = (r2 >= c) == (c2 >= c)
    zero_v = jnp.zeros((c, GLA_DV), BF16)
    st_ref[...] = jnp.zeros_like(st_ref)

    dirs = ((laf_ref, accf_ref, ci <= ri, same_head & (c2 <= r2), c - 1, c // 2 - 1),
            (lab_ref, accb_ref, ci >= ri, same_head & (c2 >= r2), 0, c // 2))
    group = 4 if nchunk % 4 == 0 else 1

    def body(i, carry):
        jobs = []
        for d, (la_ref, acc_ref, tri, keep, total_row, mid_row) in enumerate(dirs):
            for u in range(group):
                n = i * group + u
                r0 = pl.multiple_of((nchunk - 1 - n if d else n) * c, c)
                hi, mid, lo = _split3(la_ref[0, pl.ds(r0, c), :])
                tri_b = tri.astype(BF16)
                bcum = _dot(tri_b, hi) + _dot(tri_b, mid) + _dot(tri_b, lo)
                jobs.append(dict(d=d, r0=r0, bcum=bcum, keep=keep, acc=acc_ref,
                                 b_tot=bcum[total_row:total_row + 1, :],
                                 b_mid=bcum[mid_row:mid_row + 1, :]))
        for j in jobs:
            q = q_ref[0, pl.ds(j["r0"], c), :]
            k = k_ref[0, pl.ds(j["r0"], c), :]
            v = v_ref[0, pl.ds(j["r0"], c), :]
            q_in = _stack_heads(q * jnp.exp(j["bcum"] - j["b_mid"])).astype(BF16)
            k_in = _stack_heads(k * jnp.exp(j["b_mid"] - j["bcum"])).astype(BF16)
            k_dec = (k * jnp.exp(j["b_tot"] - j["bcum"])).astype(BF16)
            j["q_st"] = _stack_heads(q * jnp.exp(j["bcum"])).astype(BF16)
            j["kv_t"] = _dot(v.astype(F32).T.astype(BF16), k_dec)
            j["att"] = _dot_nt(q_in, k_in)
            j["v2"] = jnp.concatenate([jnp.concatenate([v[:, :GLA_DV], zero_v], axis=1),
                                       jnp.concatenate([zero_v, v[:, GLA_DV:]], axis=1)], axis=0)
        states = [st_ref[0], st_ref[1]]
        for j in jobs:
            state = states[j["d"]]
            j["inter"] = _dot_nt(j["q_st"], state.astype(BF16))
            states[j["d"]] = state * jnp.exp(j["b_tot"]) + j["kv_t"]
        st_ref[0], st_ref[1] = states
        for j in jobs:
            att = jnp.where(j["keep"], j["att"], 0.0).astype(BF16)
            tot = _dot(att, j["v2"]) + j["inter"]
            j["acc"][pl.ds(j["r0"], c), :] = jnp.concatenate(
                [tot[:c, :GLA_DV], tot[c:, GLA_DV:]], axis=1)
        return carry

    lax.fori_loop(0, nchunk // group, body, 0)
    for hh in range(2):
        sl = slice(hh * GLA_DV, (hh + 1) * GLA_DV)
        o_ref[0, :, sl] = _rms(accf_ref[:, sl] + accb_ref[:, sl], g_ref[:, sl]).astype(o_ref.dtype)


def _gla(gq, gk, gv, la, gain):
    b, s, _ = gq.shape
    npair = GLA_HEADS // 2
    qk = pl.BlockSpec((1, s, LANES), lambda i, p: (i, 0, p))
    return pl.pallas_call(
        functools.partial(_gla_kernel, seq=s),
        grid=(b, npair),
        in_specs=[qk, qk,
                  pl.BlockSpec((1, s, 2 * GLA_DV), lambda i, p: (i, 0, p)),
                  pl.BlockSpec((1, s, LANES), lambda i, p: (i, 0, p)),
                  pl.BlockSpec((1, s, LANES), lambda i, p: (i, 0, npair + p)),
                  pl.BlockSpec((1, 2 * GLA_DV), lambda i, p: (0, p))],
        out_specs=pl.BlockSpec((1, s, 2 * GLA_DV), lambda i, p: (i, 0, p)),
        out_shape=jax.ShapeDtypeStruct((b, s, W_D), BF16),
        scratch_shapes=[pltpu.VMEM((s, 2 * GLA_DV), F32), pltpu.VMEM((s, 2 * GLA_DV), F32),
                        pltpu.VMEM((2, 2 * GLA_DV, LANES), F32)],
        compiler_params=_params("parallel", "parallel"),
        name="gated_linear_attention",
    )(gq, gk, gv, la, la, gain)


def _merge_kernel(x_ref, g_ref, oa_ref, ob0_ref, ob1_ref, ob2_ref, l0_ref, l1_ref, l2_ref,
                  oc_ref, od_ref, wz_ref, wm_ref, bm_ref, wp_ref, wo_ref, fg_ref, perm_ref, y_ref,
                  *, final):
    x = x_ref[...]
    tm, d = x.shape
    hb = _rms(x, g_ref[...]).astype(BF16)

    obs, lses = [], []
    for (_, r), o_ref, l_ref in zip(DIL_PATTERNS, (ob0_ref, ob1_ref, ob2_ref),
                                    (l0_ref, l1_ref, l2_ref)):
        o = o_ref[0].reshape(tm, W_B)
        lse = l_ref[0].reshape(tm, W_B)
        if r > 1:
            pinv = perm_ref[_perm_slot(r)]
            o = _dot(pinv, o)
            lse = sum(_dot(pinv, part) for part in _split3(lse))
        obs.append(o.astype(F32))
        lses.append(lse)

    mx = jnp.maximum(jnp.maximum(lses[0], lses[1]), lses[2])
    es = [jnp.exp(l - mx) for l in lses]
    tot = es[0] + es[1] + es[2]
    o_b = sum(e / tot * o for e, o in zip(es, obs))

    branches = ((oa_ref[...].astype(F32), W_A), (o_b, W_B), (oc_ref[...].astype(F32), W_C),
                (od_ref[...].astype(F32), W_D))
    mixed = jnp.zeros(x.shape, F32)
    off = 0
    for i, (o, width) in enumerate(branches):
        z = _dot(hb, wz_ref[:, off:off + width])
        u = (o * (z * _sigmoid(z))).astype(BF16)
        proj = _dot(u, wp_ref[off:off + width, :])
        gate = _sigmoid(_dot(hb, wm_ref[:, i * d:(i + 1) * d]) + bm_ref[:, i * d:(i + 1) * d])
        mixed = mixed + gate * proj
        off += width
    y = x + _dot(mixed.astype(BF16), wo_ref[...])
    if final:
        y = _rms(y, fg_ref[...])
    y_ref[...] = y


def _merge(x2, seq, g, oa, obs, lses, oc, od, wz, wm, bm, wp, wo, fg, final, tm):
    n, d = x2.shape
    nseq = seq // tm
    row = lambda width: pl.BlockSpec((tm, width), lambda i: (i, 0))
    sub = lambda t: pl.BlockSpec((1, t.shape[1], tm // t.shape[1], W_B),
                                 lambda i: (i // nseq, 0, i % nseq, 0))
    consts = (wz, wm, bm, wp, wo, fg, _perm_tables(tm, True))
    return pl.pallas_call(
        functools.partial(_merge_kernel, final=final),
        grid=(n // tm,),
        in_specs=[row(d), _const_spec(g.shape), row(W_A)] + [sub(t) for t in (*obs, *lses)]
                 + [row(W_C), row(W_D)] + [_const_spec(a.shape) for a in consts],
        out_specs=row(d),
        out_shape=jax.ShapeDtypeStruct((n, d), F32),
        compiler_params=_params("parallel"),
        name="gate_merge_out",
    )(x2, g, oa, *obs, *lses, oc, od, *consts)


def _rope_tables(seq):
    half = MLA_ROPE // 2
    inv = jnp.power(ROPE_THETA, -jnp.arange(0, MLA_ROPE, 2, dtype=F32) / MLA_ROPE)
    ang = jnp.arange(seq, dtype=jnp.int32).astype(F32)[:, None] * inv[None, :]
    cos, sin = jnp.cos(ang), jnp.sin(ang)
    zero = jnp.zeros((seq, half), F32)
    cos_t = jnp.concatenate([cos] * 4, axis=-1)
    sin_hi = jnp.concatenate([zero, sin, zero, sin], axis=-1)
    sin_lo = jnp.concatenate([-sin, zero, -sin, zero], axis=-1)
    return cos_t, sin_hi, sin_lo


def _prep_layer(w_in, w_uq, w_ukv, w_gf, b_gf, w_gb, b_gb):
    d = w_in.shape[0]
    o = _IN_OFF
    zpad = lambda width: jnp.zeros((d, width), w_in.dtype)
    w_pad = jnp.concatenate([
        w_in[:, o[0]:o[3]], zpad(LANES - MLA_ROPE),
        w_in[:, o[3]:o[10]],
        w_in[:, o[10]:o[12]], zpad(LANES - 2 * GLA_GATE_RANK)], axis=1)
    w_z = w_in[:, o[12]:o[16]]
    uq = w_uq.reshape(MLA_Q_RANK, MLA_HEADS, MLA_NOPE + MLA_ROPE)
    uq = jnp.pad(uq, ((0, 0), (0, 0), (0, MLA_HEAD_STRIDE - MLA_NOPE - MLA_ROPE)))
    uq = uq.reshape(MLA_Q_RANK, MLA_HEADS * MLA_HEAD_STRIDE).astype(BF16)
    ukv = w_ukv.reshape(MLA_KV_RANK, MLA_HEADS, MLA_NOPE + MLA_V)
    ukv = jnp.concatenate([ukv[:, :, :MLA_NOPE].reshape(MLA_KV_RANK, -1),
                           ukv[:, :, MLA_NOPE:].reshape(MLA_KV_RANK, -1)], axis=1).astype(BF16)
    wgate = jnp.zeros((LANES, 2 * W_GLA_QK), F32)
    wgate = wgate.at[:GLA_GATE_RANK, :W_GLA_QK].set(w_gf)
    wgate = wgate.at[GLA_GATE_RANK:2 * GLA_GATE_RANK, W_GLA_QK:].set(w_gb)
    bgate = jnp.concatenate([b_gf, b_gb])[None, :]
    return w_pad, w_z, uq, ukv, wgate.astype(BF16), bgate


def kernel(x, norm_g, w_in, mla_q_norm_g, mla_w_uq, mla_kv_norm_g, mla_w_ukv, nat_rpb,
           gla_w_gate_f, gla_b_gate_f, gla_w_gate_b, gla_b_gate_b, gla_norm_g,
           w_proj_a, w_proj_b, w_proj_c, w_proj_d, w_merge, b_merge, w_out, final_norm_g):
    b, s, d = x.shape
    depth = w_in.shape[0]
    tm = min(512, s)
    tq = min(512, s)
    tabs = _rope_tables(s)
    x2 = x.reshape(b * s, d)
    w_in_b = w_in.astype(BF16)
    for l in range(depth):
        w_pad, w_z, uq, ukv, wgate, bgate = _prep_layer(
            w_in_b[l], mla_w_uq[l], mla_w_ukv[l], gla_w_gate_f[l], gla_b_gate_f[l],
            gla_w_gate_b[l], gla_b_gate_b[l])
        (q, kc, va, d0, d1, d2, nat_qkv, gq, gk, gv, la) = _in_proj(
            x2, s, tabs, norm_g[l][None, :], w_pad, mla_q_norm_g[l][None, :], uq,
            mla_kv_norm_g[l][None, :], ukv, wgate, bgate, tm)
        sh = lambda t: t.reshape(b, s, t.shape[-1])
        o_a = _mla(sh(q), sh(kc), sh(va), tq)
        obs, lses = [], []
        for dil_qkv, (window, _) in zip((d0, d1, d2), DIL_PATTERNS):
            o_g, lse_g = _dilated_group(dil_qkv, window)
            obs.append(o_g)
            lses.append(lse_g)
        o_c = _nat(sh(nat_qkv), nat_rpb[l])
        o_d = _gla(sh(gq), sh(gk), sh(gv), sh(la), gla_norm_g[l].reshape(1, W_D))
        w_p = jnp.concatenate([w_proj_a[l], w_proj_b[l], w_proj_c[l], w_proj_d[l]], axis=0)
        x2 = _merge(x2, s, norm_g[l][None, :], o_a.reshape(b * s, W_A), obs, lses,
                    o_c.reshape(b * s, W_C), o_d.reshape(b * s, W_D), w_z,
                    w_merge[l].astype(BF16), b_merge[l][None, :], w_p.astype(BF16),
                    w_out[l].astype(BF16), final_norm_g[None, :], l == depth - 1, tm)
    return x2.reshape(b, s, d)
```

```python
import functools

import numpy as np
import jax
import jax.numpy as jnp
from jax import lax
from jax.experimental import pallas as pl
from jax.experimental.pallas import tpu as pltpu

F32 = jnp.float32
BF16 = jnp.bfloat16

GRID_W = 64
ROPE_THETA = 10000.0
NORM_EPS = 1e-6
NEG_INF = -1e30

MLA_HEADS = 4
MLA_NOPE = 128
MLA_ROPE = 64
MLA_V = 128
MLA_Q_RANK = 256
MLA_KV_RANK = 128
DIL_PATTERNS = ((128, 1), (512, 4), (2048, 16))
DIL_HEADS = 4
DIL_DH = 64
NAT_HEADS = 8
NAT_DH = 64
NAT_KH_MAX = 8
NAT_KW = 16
GLA_HEADS = 4
GLA_DK = 64
GLA_DV = 128
GLA_GATE_RANK = 16
GLA_TAU = 16.0
GLA_CHUNK = 64

W_A = MLA_HEADS * MLA_V
W_B = DIL_HEADS * DIL_DH
W_C = NAT_HEADS * NAT_DH
W_D = GLA_HEADS * GLA_DV
N_BRANCH = 4
W_DIL_GROUP = 3 * W_B
W_GLA_QK = GLA_HEADS * GLA_DK

LOG2_E = 1.4426950408889634
LANES = 128
MLA_HEAD_STRIDE = 256
MLA_SUB_QUERIES = 256
VMEM_LIMIT = 56 * 1024 * 1024

_IN_SIZES = (MLA_Q_RANK, MLA_KV_RANK, MLA_ROPE, 3 * W_DIL_GROUP, W_C, W_C, W_C,
             W_GLA_QK, W_GLA_QK, W_D, GLA_GATE_RANK, GLA_GATE_RANK, W_A, W_B, W_C, W_D)
_IN_OFF = tuple(int(v) for v in np.concatenate([[0], np.cumsum(_IN_SIZES)]))

_P_Q = 0
_P_KV = _P_Q + MLA_Q_RANK
_P_KR = _P_KV + MLA_KV_RANK
_P_DIL = _P_KR + LANES
_P_NAT = _P_DIL + 3 * W_DIL_GROUP
_P_GQ = _P_NAT + 3 * W_C
_P_GK = _P_GQ + W_GLA_QK
_P_GV = _P_GK + W_GLA_QK
_P_GG = _P_GV + W_D
_P_END = _P_GG + LANES


def _dot(a, b):
    return jnp.dot(a, b, preferred_element_type=F32)


def _dot_nt(a, b):
    return lax.dot_general(a, b, (((1,), (1,)), ((), ())), preferred_element_type=F32)


def _rms(x, g):
    return x * lax.rsqrt(jnp.mean(x * x, axis=-1, keepdims=True) + NORM_EPS) * g


def _rope_cols(w):
    half = MLA_ROPE // 2
    t = w.reshape(w.shape[:-1] + (w.shape[-1] // LANES, 2, 2, half))
    return jnp.swapaxes(t, -3, -2).reshape(w.shape)


def _rope128(x, cos, sin_signed):
    return x * cos + pltpu.roll(x, LANES // 2, 1) * sin_signed


def _sigmoid(x):
    return 1.0 / (1.0 + jnp.exp(-x))


def _log_sigmoid(x):
    return jnp.minimum(x, 0.0) - jnp.log1p(jnp.exp(-jnp.abs(x)))


def _params(*sem):
    return pltpu.CompilerParams(dimension_semantics=sem, vmem_limit_bytes=VMEM_LIMIT)


def _const_spec(shape):
    nd = len(shape)
    return pl.BlockSpec(shape, lambda *_: (0,) * nd, pipeline_mode=pl.Buffered(1))


_PERM_DILATIONS = tuple(r for _, r in DIL_PATTERNS if r > 1)


def _perm_slot(r):
    return _PERM_DILATIONS.index(r)


PERM_BLOCK = 256


def _perm_tables(inverse):
    tabs = []
    for r in _PERM_DILATIONS:
        p = np.zeros((PERM_BLOCK, PERM_BLOCK), np.float32)
        tok = np.arange(PERM_BLOCK)
        p[(tok % r) * (PERM_BLOCK // r) + tok // r, tok] = 1.0
        tabs.append(p.T if inverse else p)
    return jnp.asarray(np.stack(tabs), BF16)


def _in_proj_kernel(x_ref, cos_ref, sin_ref, g_ref, w_ref, qg_ref, wuq_ref, kvg_ref,
                    wukv_ref, wgate_ref, bgate_ref, perm_ref,
                    q_ref, kc_ref, va_ref, d0_ref, d1_ref, d2_ref, nat_ref, gq_ref, gk_ref,
                    gv_ref, la_ref, y_ref):
    tm = x_ref.shape[0]
    hb = _rms(x_ref[...], g_ref[...]).astype(BF16)
    cos, sin = cos_ref[...], sin_ref[...]

    def mm(a, width):
        return _dot(hb, w_ref[:, a:a + width])

    def rope(t):
        return _rope128(t, cos, sin)

    qn = _rms(mm(_P_Q, MLA_Q_RANK), qg_ref[...]).astype(BF16)
    for h in range(MLA_HEADS):
        base = h * MLA_HEAD_STRIDE
        qh = _dot(qn, wuq_ref[:, base:base + MLA_HEAD_STRIDE])
        q_ref[:, base:base + LANES] = qh[:, :LANES].astype(BF16)
        q_ref[:, base + LANES:base + 2 * LANES] = rope(qh[:, LANES:]).astype(BF16)
    kv_kr = mm(_P_KV, MLA_KV_RANK + LANES)
    kvn = _rms(kv_kr[:, :MLA_KV_RANK], kvg_ref[...]).astype(BF16)
    kpe = rope(kv_kr[:, MLA_KV_RANK:]).astype(BF16)
    k_nope = _dot(kvn, wukv_ref[:, :W_A])
    for h in range(MLA_HEADS):
        base = h * MLA_HEAD_STRIDE
        kc_ref[:, base:base + LANES] = k_nope[:, h * LANES:(h + 1) * LANES].astype(BF16)
        kc_ref[:, base + LANES:base + 2 * LANES] = kpe
    va_ref[...] = _dot(kvn, wukv_ref[:, W_A:2 * W_A]).astype(BF16)

    for g, (d_ref, (_, r)) in enumerate(zip((d0_ref, d1_ref, d2_ref), DIL_PATTERNS)):
        base = _P_DIL + g * W_DIL_GROUP
        qk = mm(base, 2 * W_B)
        for c in range(0, 2 * W_B, LANES):
            fold = DIL_DH ** -0.5 if c < W_B else 1.0
            y_ref[g, :, c:c + LANES] = (rope(qk[:, c:c + LANES]) * fold).astype(BF16)
        y_ref[g, :, 2 * W_B:] = mm(base + 2 * W_B, W_B).astype(BF16)
        if r == 1:
            d_ref[0, 0] = y_ref[g]
            continue
        per = PERM_BLOCK // r
        for blk in range(tm // PERM_BLOCK):
            y = y_ref[g, blk * PERM_BLOCK:(blk + 1) * PERM_BLOCK, :]
            y = _dot(perm_ref[_perm_slot(r)], y).astype(BF16)
            d_ref[0, :, blk * per:(blk + 1) * per, :] = y.reshape(r, per, W_DIL_GROUP)

    nat_ref[:, :W_C] = (mm(_P_NAT, W_C) * (NAT_DH ** -0.5)).astype(BF16)
    for c in range(W_C, 3 * W_C, W_C):
        nat_ref[:, c:c + W_C] = mm(_P_NAT + c, W_C).astype(BF16)

    gq_ref[...] = mm(_P_GQ, W_GLA_QK) * (GLA_DK ** -0.5)
    gk_ref[...] = mm(_P_GK, W_GLA_QK)
    gv_ref[...] = mm(_P_GV, W_D).astype(BF16)
    gates = _dot(mm(_P_GG, LANES).astype(BF16), wgate_ref[...]) + bgate_ref[...]
    la_ref[...] = _log_sigmoid(gates) * (1.0 / GLA_TAU)


def _in_proj(x2, seq, tabs, g, w, qg, wuq, kvg, wukv, wgate, bgate, tm):
    n, d = x2.shape
    nseq = seq // tm
    batch = n // seq
    row = lambda width: pl.BlockSpec((tm, width), lambda i: (i, 0))
    tab = pl.BlockSpec((tm, LANES), lambda i: (i % nseq, 0))
    sub = lambda r, width: pl.BlockSpec((1, r, tm // r, width),
                                        lambda i: (i // nseq, 0, i % nseq, 0))
    sub_shape = lambda r, width, dt: jax.ShapeDtypeStruct((batch, r, seq // r, width), dt)
    dil = [r for _, r in DIL_PATTERNS]
    outs = [(4 * MLA_HEAD_STRIDE, BF16), (4 * MLA_HEAD_STRIDE, BF16), (W_A, BF16)]
    outs2 = [(3 * W_C, BF16), (W_GLA_QK, F32), (W_GLA_QK, F32), (W_D, BF16), (2 * W_GLA_QK, F32)]
    assert tm % PERM_BLOCK == 0
    perm = _perm_tables(False)
    consts = (g, w, qg, wuq, kvg, wukv, wgate, bgate, perm)
    return pl.pallas_call(
        _in_proj_kernel,
        grid=(n // tm,),
        in_specs=[row(d), tab, tab] + [_const_spec(a.shape) for a in consts],
        out_specs=([row(wd) for wd, _ in outs] + [sub(r, W_DIL_GROUP) for r in dil]
                   + [row(wd) for wd, _ in outs2]),
        out_shape=([jax.ShapeDtypeStruct((n, wd), dt) for wd, dt in outs]
                   + [sub_shape(r, W_DIL_GROUP, BF16) for r in dil]
                   + [jax.ShapeDtypeStruct((n, wd), dt) for wd, dt in outs2]),
        scratch_shapes=[pltpu.VMEM((len(dil), tm, W_DIL_GROUP), BF16)],
        compiler_params=_params("parallel"),
        name="in_proj",
    )(x2, *tabs, *consts)


def _mla_kernel(q_ref, k_ref, v_ref, o_ref, vx_ref):
    @pl.when(pl.program_id(2) == 0)
    def _():
        vx_ref[:, :MLA_V] = v_ref[0]
        vx_ref[:, MLA_V:] = jnp.ones((vx_ref.shape[0], MLA_V), BF16)

    c = (MLA_NOPE + MLA_ROPE) ** -0.5 * LOG2_E
    sub = MLA_SUB_QUERIES
    nsub = q_ref.shape[1] // sub
    scores = [_dot_nt(q_ref[0, i * sub:(i + 1) * sub, :], k_ref[0]) for i in range(nsub)]
    for i, s in enumerate(scores):
        m = jnp.max(s, axis=-1, keepdims=True)
        p = jnp.exp2((s - m) * c).astype(BF16)
        ov = _dot(p, vx_ref[...])
        o_ref[0, i * sub:(i + 1) * sub, :] = (ov[:, :MLA_V] / ov[:, MLA_V:]).astype(o_ref.dtype)


def _mla(q, kc, va, tq):
    b, s, _ = q.shape
    return pl.pallas_call(
        _mla_kernel,
        grid=(b, MLA_HEADS, s // tq),
        in_specs=[pl.BlockSpec((1, tq, MLA_HEAD_STRIDE), lambda i, h, j: (i, j, h)),
                  pl.BlockSpec((1, s, MLA_HEAD_STRIDE), lambda i, h, j: (i, 0, h)),
                  pl.BlockSpec((1, s, MLA_V), lambda i, h, j: (i, 0, h))],
        out_specs=pl.BlockSpec((1, tq, MLA_V), lambda i, h, j: (i, j, h)),
        out_shape=jax.ShapeDtypeStruct((b, s, W_A), BF16),
        scratch_shapes=[pltpu.VMEM((s, 2 * MLA_V), BF16)],
        compiler_params=_params("parallel", "parallel", "arbitrary"),
        name="mla_attention",
    )(q, kc, va)


def _stack_heads(x128, rope_order=False):
    lane = lax.broadcasted_iota(jnp.int32, x128.shape, 1)
    first = ((lane & 32) == 0) if rope_order else (lane < 64)
    zero = jnp.zeros_like(x128)
    return jnp.concatenate([jnp.where(first, x128, zero), jnp.where(first, zero, x128)], axis=0)


def _unstack_heads(y2):
    m = y2.shape[0] // 2
    lane = lax.broadcasted_iota(jnp.int32, (m, y2.shape[1]), 1)
    return jnp.where(lane < 64, y2[:m], y2[m:])


def _pair_scores(q128, k128, rope_order=False):
    return _dot_nt(_stack_heads(q128, rope_order), k128)


def _pair_softmax_pv(s, bias, v128, want_lse):
    s = s + bias
    mx = jnp.max(s, axis=-1, keepdims=True)
    p = jnp.exp(s - mx)
    den = jnp.sum(p, axis=-1, keepdims=True)
    o = _unstack_heads(_dot(p.astype(BF16), v128) / den)
    if not want_lse:
        return o, None
    lse = jnp.broadcast_to(mx + jnp.log(den), (s.shape[0], LANES))
    return o, _unstack_heads(lse)


def _dil_kernel(qkv_ref, bias_ref, o_ref, lse_ref, s_ref, *, length, qb, win, radius, group):
    per_seq = length // (qb * group)

    def body(i, carry):
        sub = i // per_seq
        jobs = []
        for u in range(group):
            q0 = pl.multiple_of(((i % per_seq) * group + u) * qb, qb)
            ks = pl.multiple_of(jnp.clip(q0 - radius, 0, length - win), 64)
            for c in range(0, W_B, LANES):
                q128 = qkv_ref[sub, pl.ds(q0, qb), c:c + LANES]
                k128 = qkv_ref[sub, pl.ds(ks, win), W_B + c:W_B + c + LANES]
                s_ref[len(jobs)] = _pair_scores(q128, k128, rope_order=True)
                jobs.append((q0, ks, c))
        for slot, (q0, ks, c) in enumerate(jobs):
            v128 = qkv_ref[sub, pl.ds(ks, win), 2 * W_B + c:2 * W_B + c + LANES]
            o, lse = _pair_softmax_pv(s_ref[slot], bias_ref[(q0 - ks) // 64], v128, True)
            o_ref[sub, pl.ds(q0, qb), c:c + LANES] = o.astype(o_ref.dtype)
            lse_ref[sub, pl.ds(q0, qb), c:c + LANES] = lse
        return carry

    lax.fori_loop(0, qkv_ref.shape[0] * per_seq, body, 0)


def _dil_bias_table(qb, win, radius):
    nvar = (win - qb) // 64 + 1
    i = (np.arange(2 * qb) % qb)[None, :, None]
    j = np.arange(win)[None, None, :]
    delta = (64 * np.arange(nvar))[:, None, None]
    return jnp.asarray(np.where(np.abs(i + delta - j) <= radius, 0.0, NEG_INF), F32)


def _dilated_group(qkv, window):
    b, r, length, _ = qkv.shape
    radius = window // (2 * r)
    qb = min(128, length)
    win = min(qb + 2 * radius, length)
    assert length % qb == 0 and radius % 64 == 0 and qb % 64 == 0
    table = _dil_bias_table(qb, win, radius)
    group = 2 if (length // qb) % 2 == 0 else 1
    nsub = max(1, r // 4)
    sub = lambda width: pl.BlockSpec((None, nsub, length, width), lambda i, m: (i, m, 0, 0))
    return pl.pallas_call(
        functools.partial(_dil_kernel, length=length, qb=qb, win=win, radius=radius, group=group),
        grid=(b, r // nsub),
        in_specs=[sub(W_DIL_GROUP), _const_spec(table.shape)],
        out_specs=[sub(W_B), sub(W_B)],
        out_shape=[jax.ShapeDtypeStruct((b, r, length, W_B), BF16),
                   jax.ShapeDtypeStruct((b, r, length, W_B), F32)],
        scratch_shapes=[pltpu.VMEM((group * DIL_HEADS // 2, 2 * qb, win), F32)],
        compiler_params=_params("parallel", "parallel"),
        name=f"dilated_attention_r{r}",
    )(qkv, table)


def _nat_bias_table(rpb):
    qc = np.arange(GRID_W)[:, None]
    kc = np.arange(GRID_W)[None, :]
    win_start = np.clip(qc - NAT_KW // 2, 0, GRID_W - NAT_KW)
    allowed = (kc >= win_start) & (kc < win_start + NAT_KW)
    n_row = 2 * NAT_KH_MAX - 1
    side = GRID_W - NAT_KW
    g = jnp.pad(rpb, ((0, 0), (0, 0), (side, side + 1)))
    a = jnp.broadcast_to(g[:, :, None, :], (NAT_HEADS, n_row, GRID_W, 2 * GRID_W))
    a = a.reshape(NAT_HEADS, n_row, 2 * GRID_W * GRID_W)
    a = a[:, :, GRID_W - 1:GRID_W - 1 + GRID_W * (2 * GRID_W - 1)]
    toep = a.reshape(NAT_HEADS, n_row, GRID_W, 2 * GRID_W - 1)[..., :GRID_W]
    toep = jnp.where(allowed[None, None], toep, NEG_INF).astype(F32)
    return jnp.concatenate([toep[:, :-1], toep[:, 1:]], axis=-1)


def _nat_kernel(q_ref, k_ref, v_ref, bias_ref, o_ref, s_ref, *, rows, group):
    kh = NAT_KH_MAX

    def body(i, carry):
        jobs = []
        for u in range(group):
            r = i * group + u
            rs = jnp.clip(r - kh // 2, 0, rows - kh)
            q0 = pl.multiple_of(r * GRID_W, GRID_W)
            k0 = pl.multiple_of(rs * GRID_W, GRID_W)
            s_ref[u] = _pair_scores(q_ref[0, pl.ds(q0, GRID_W), :],
                                    k_ref[0, pl.ds(k0, kh * GRID_W), :])
            jobs.append((q0, k0, rs - r + NAT_KH_MAX - 1))
        for u, (q0, k0, var) in enumerate(jobs):
            v128 = v_ref[0, pl.ds(k0, kh * GRID_W), :]
            bias = jnp.concatenate(
                [jnp.concatenate([bias_ref[hh, var + j] for j in range(0, kh, 2)], axis=1)
                 for hh in range(2)], axis=0)
            o, _ = _pair_softmax_pv(s_ref[u], bias, v128, False)
            o_ref[0, pl.ds(q0, GRID_W), :] = o.astype(o_ref.dtype)
        return carry

    lax.fori_loop(0, rows // group, body, 0)


def _nat(qkv, rpb):
    b, s, _ = qkv.shape
    rows = s // GRID_W
    assert rows >= NAT_KH_MAX and NAT_KH_MAX % 2 == 0
    table = _nat_bias_table(rpb)
    npair = NAT_HEADS // 2
    col = lambda off: pl.BlockSpec((1, s, LANES), lambda i, p: (i, 0, off + p))
    group = 8 if rows % 8 == 0 else 1
    return pl.pallas_call(
        functools.partial(_nat_kernel, rows=rows, group=group),
        grid=(b, npair),
        in_specs=[col(0), col(npair), col(2 * npair),
                  pl.BlockSpec((2,) + table.shape[1:], lambda i, p: (p, 0, 0, 0))],
        out_specs=pl.BlockSpec((1, s, LANES), lambda i, p: (i, 0, p)),
        out_shape=jax.ShapeDtypeStruct((b, s, W_C), BF16),
        scratch_shapes=[pltpu.VMEM((group, 2 * GRID_W, NAT_KH_MAX * GRID_W), F32)],
        compiler_params=_params("parallel", "parallel"),
        name="neighbourhood_attention",
    )(qkv, qkv, qkv, table)


def _split3(x):
    hi = x.astype(BF16)
    r1 = x - hi.astype(F32)
    mid = r1.astype(BF16)
    lo = (r1 - mid.astype(F32)).astype(BF16)
    return hi, mid, lo


def _gla_kernel(q_ref, k_ref, v_ref, laf_ref, lab_ref, g_ref, o_ref, accf_ref, accb_ref, st_ref,
                *, seq):
    c = GLA_CHUNK
    nchunk = seq // c
    ri = lax.broadcasted_iota(jnp.int32, (c, c), 0)
    ci = lax.broadcasted_iota(jnp.int32, (c, c), 1)
    r2 = lax.broadcasted_iota(jnp.int32, (2 * c, 2 * c), 0)
    c2 = lax.broadcasted_iota(jnp.int32, (2 * c, 2 * c), 1)
    same_head = (r2 >= c) == (c2 >= c)
    zero_v = jnp.zeros((c, GLA_DV), BF16)
    st_ref[...] = jnp.zeros_like(st_ref)

    dirs = ((laf_ref, accf_ref, ci <= ri, same_head & (c2 <= r2), c - 1, c // 2 - 1),
            (lab_ref, accb_ref, ci >= ri, same_head & (c2 >= r2), 0, c // 2))
    group = 4 if nchunk % 4 == 0 else 1

    def body(i, carry):
        jobs = []
        for d, (la_ref, acc_ref, tri, keep, total_row, mid_row) in enumerate(dirs):
            for u in range(group):
                n = i * group + u
                r0 = pl.multiple_of((nchunk - 1 - n if d else n) * c, c)
                hi, mid, lo = _split3(la_ref[0, pl.ds(r0, c), :])
                tri_b = tri.astype(BF16)
                bcum = _dot(tri_b, hi) + _dot(tri_b, mid) + _dot(tri_b, lo)
                jobs.append(dict(d=d, r0=r0, bcum=bcum, keep=keep, acc=acc_ref,
                                 b_tot=bcum[total_row:total_row + 1, :],
                                 b_mid=bcum[mid_row:mid_row + 1, :]))
        for j in jobs:
            q = q_ref[0, pl.ds(j["r0"], c), :]
            k = k_ref[0, pl.ds(j["r0"], c), :]
            v = v_ref[0, pl.ds(j["r0"], c), :]
            q_in = _stack_heads(q * jnp.exp(j["bcum"] - j["b_mid"])).astype(BF16)
            k_in = _stack_heads(k * jnp.exp(j["b_mid"] - j["bcum"])).astype(BF16)
            k_dec = (k * jnp.exp(j["b_tot"] - j["bcum"])).astype(BF16)
            j["q_st"] = _stack_heads(q * jnp.exp(j["bcum"])).astype(BF16)
            j["kv_t"] = _dot(v.astype(F32).T.astype(BF16), k_dec)
            j["att"] = _dot_nt(q_in, k_in)
            j["v2"] = jnp.concatenate([jnp.concatenate([v[:, :GLA_DV], zero_v], axis=1),
                                       jnp.concatenate([zero_v, v[:, GLA_DV:]], axis=1)], axis=0)
        states = [st_ref[0], st_ref[1]]
        for j in jobs:
            state = states[j["d"]]
            j["inter"] = _dot_nt(j["q_st"], state.astype(BF16))
            states[j["d"]] = state * jnp.exp(j["b_tot"]) + j["kv_t"]
        st_ref[0], st_ref[1] = states
        for j in jobs:
            att = jnp.where(j["keep"], j["att"], 0.0).astype(BF16)
            tot = _dot(att, j["v2"]) + j["inter"]
            j["acc"][pl.ds(j["r0"], c), :] = jnp.concatenate(
                [tot[:c, :GLA_DV], tot[c:, GLA_DV:]], axis=1)
        return carry

    lax.fori_loop(0, nchunk // group, body, 0)
    for hh in range(2):
        sl = slice(hh * GLA_DV, (hh + 1) * GLA_DV)
        o_ref[0, :, sl] = _rms(accf_ref[:, sl] + accb_ref[:, sl], g_ref[:, sl]).astype(o_ref.dtype)


def _gla(gq, gk, gv, la, gain):
    b, s, _ = gq.shape
    npair = GLA_HEADS // 2
    qk = pl.BlockSpec((1, s, LANES), lambda i, p: (i, 0, p))
    return pl.pallas_call(
        functools.partial(_gla_kernel, seq=s),
        grid=(b, npair),
        in_specs=[qk, qk,
                  pl.BlockSpec((1, s, 2 * GLA_DV), lambda i, p: (i, 0, p)),
                  pl.BlockSpec((1, s, LANES), lambda i, p: (i, 0, p)),
                  pl.BlockSpec((1, s, LANES), lambda i, p: (i, 0, npair + p)),
                  pl.BlockSpec((1, 2 * GLA_DV), lambda i, p: (0, p))],
        out_specs=pl.BlockSpec((1, s, 2 * GLA_DV), lambda i, p: (i, 0, p)),
        out_shape=jax.ShapeDtypeStruct((b, s, W_D), BF16),
        scratch_shapes=[pltpu.VMEM((s, 2 * GLA_DV), F32), pltpu.VMEM((s, 2 * GLA_DV), F32),
                        pltpu.VMEM((2, 2 * GLA_DV, LANES), F32)],
        compiler_params=_params("parallel", "parallel"),
        name="gated_linear_attention",
    )(gq, gk, gv, la, la, gain)


def _merge_kernel(x_ref, g_ref, oa_ref, ob0_ref, ob1_ref, ob2_ref, l0_ref, l1_ref, l2_ref,
                  oc_ref, od_ref, wz_ref, wm_ref, bm_ref, wp_ref, wo_ref, fg_ref, perm_ref, y_ref,
                  *, final):
    x = x_ref[...]
    tm, d = x.shape
    hb = _rms(x, g_ref[...]).astype(BF16)

    obs, lses = [], []
    for (_, r), o_ref, l_ref in zip(DIL_PATTERNS, (ob0_ref, ob1_ref, ob2_ref),
                                    (l0_ref, l1_ref, l2_ref)):
        if r == 1:
            obs.append(o_ref[0, 0].astype(F32))
            lses.append(l_ref[0, 0])
            continue
        pinv = perm_ref[_perm_slot(r)]
        per = PERM_BLOCK // r
        o_blocks, l_blocks = [], []
        for blk in range(tm // PERM_BLOCK):
            rows = slice(blk * per, (blk + 1) * per)
            o_blocks.append(_dot(pinv, o_ref[0, :, rows, :].reshape(PERM_BLOCK, W_B)))
            parts = _split3(l_ref[0, :, rows, :].reshape(PERM_BLOCK, W_B))
            l_blocks.append(sum(_dot(pinv, part) for part in parts))
        obs.append(jnp.concatenate(o_blocks, axis=0))
        lses.append(jnp.concatenate(l_blocks, axis=0))

    mx = jnp.maximum(jnp.maximum(lses[0], lses[1]), lses[2])
    es = [jnp.exp(l - mx) for l in lses]
    tot = es[0] + es[1] + es[2]
    o_b = sum(e / tot * o for e, o in zip(es, obs))

    branches = ((oa_ref[...].astype(F32), W_A), (o_b, W_B), (oc_ref[...].astype(F32), W_C),
                (od_ref[...].astype(F32), W_D))
    mixed = jnp.zeros(x.shape, F32)
    off = 0
    for i, (o, width) in enumerate(branches):
        z = _dot(hb, wz_ref[:, off:off + width])
        u = (o * (z * _sigmoid(z))).astype(BF16)
        proj = _dot(u, wp_ref[off:off + width, :])
        gate = _sigmoid(_dot(hb, wm_ref[:, i * d:(i + 1) * d]) + bm_ref[:, i * d:(i + 1) * d])
        mixed = mixed + gate * proj
        off += width
    y = x + _dot(mixed.astype(BF16), wo_ref[...])
    if final:
        y = _rms(y, fg_ref[...])
    y_ref[...] = y


def _merge(x2, seq, g, oa, obs, lses, oc, od, wz, wm, bm, wp, wo, fg, final, tm):
    n, d = x2.shape
    nseq = seq // tm
    row = lambda width: pl.BlockSpec((tm, width), lambda i: (i, 0))
    sub = lambda t: pl.BlockSpec((1, t.shape[1], tm // t.shape[1], W_B),
                                 lambda i: (i // nseq, 0, i % nseq, 0))
    consts = (wz, wm, bm, wp, wo, fg, _perm_tables(True))
    return pl.pallas_call(
        functools.partial(_merge_kernel, final=final),
        grid=(n // tm,),
        in_specs=[row(d), _const_spec(g.shape), row(W_A)] + [sub(t) for t in (*obs, *lses)]
                 + [row(W_C), row(W_D)] + [_const_spec(a.shape) for a in consts],
        out_specs=row(d),
        out_shape=jax.ShapeDtypeStruct((n, d), F32),
        compiler_params=_params("parallel"),
        name="gate_merge_out",
    )(x2, g, oa, *obs, *lses, oc, od, *consts)


def _rope_tables(seq):
    half = MLA_ROPE // 2
    inv = jnp.power(ROPE_THETA, -jnp.arange(0, MLA_ROPE, 2, dtype=F32) / MLA_ROPE)
    ang = jnp.arange(seq, dtype=jnp.int32).astype(F32)[:, None] * inv[None, :]
    cos, sin = jnp.cos(ang), jnp.sin(ang)
    return jnp.concatenate([cos] * 4, axis=-1), jnp.concatenate([-sin, -sin, sin, sin], axis=-1)


def _prep_layer(w_in, w_uq, w_ukv, w_gf, b_gf, w_gb, b_gb):
    d = w_in.shape[0]
    o = _IN_OFF
    zpad = lambda width: jnp.zeros((d, width), w_in.dtype)
    dil = w_in[:, o[3]:o[4]].reshape(d, len(DIL_PATTERNS), 3, W_B)
    dil = jnp.concatenate([_rope_cols(dil[:, :, :2]), dil[:, :, 2:]], axis=2)
    w_pad = jnp.concatenate([
        w_in[:, o[0]:o[2]],
        _rope_cols(jnp.concatenate([w_in[:, o[2]:o[3]], zpad(LANES - MLA_ROPE)], axis=1)),
        dil.reshape(d, 3 * W_DIL_GROUP),
        w_in[:, o[4]:o[10]],
        w_in[:, o[10]:o[12]], zpad(LANES - 2 * GLA_GATE_RANK)], axis=1)
    w_z = w_in[:, o[12]:o[16]]
    uq = w_uq.reshape(MLA_Q_RANK, MLA_HEADS, MLA_NOPE + MLA_ROPE)
    uq = jnp.pad(uq, ((0, 0), (0, 0), (0, MLA_HEAD_STRIDE - MLA_NOPE - MLA_ROPE)))
    uq = jnp.concatenate([uq[:, :, :MLA_NOPE], _rope_cols(uq[:, :, MLA_NOPE:])], axis=2)
    uq = uq.reshape(MLA_Q_RANK, MLA_HEADS * MLA_HEAD_STRIDE).astype(BF16)
    ukv = w_ukv.reshape(MLA_KV_RANK, MLA_HEADS, MLA_NOPE + MLA_V)
    ukv = jnp.concatenate([ukv[:, :, :MLA_NOPE].reshape(MLA_KV_RANK, -1),
                           ukv[:, :, MLA_NOPE:].reshape(MLA_KV_RANK, -1)], axis=1).astype(BF16)
    wgate = jnp.zeros((LANES, 2 * W_GLA_QK), F32)
    wgate = wgate.at[:GLA_GATE_RANK, :W_GLA_QK].set(w_gf)
    wgate = wgate.at[GLA_GATE_RANK:2 * GLA_GATE_RANK, W_GLA_QK:].set(w_gb)
    bgate = jnp.concatenate([b_gf, b_gb])[None, :]
    return w_pad, w_z, uq, ukv, wgate.astype(BF16), bgate


def kernel(x, norm_g, w_in, mla_q_norm_g, mla_w_uq, mla_kv_norm_g, mla_w_ukv, nat_rpb,
           gla_w_gate_f, gla_b_gate_f, gla_w_gate_b, gla_b_gate_b, gla_norm_g,
           w_proj_a, w_proj_b, w_proj_c, w_proj_d, w_merge, b_merge, w_out, final_norm_g):
    b, s, d = x.shape
    depth = w_in.shape[0]
    tm = min(512, s)
    tq = min(1024, s)
    tabs = _rope_tables(s)
    x2 = x.reshape(b * s, d)
    w_in_b = w_in.astype(BF16)
    for l in range(depth):
        w_pad, w_z, uq, ukv, wgate, bgate = _prep_layer(
            w_in_b[l], mla_w_uq[l], mla_w_ukv[l], gla_w_gate_f[l], gla_b_gate_f[l],
            gla_w_gate_b[l], gla_b_gate_b[l])
        (q, kc, va, d0, d1, d2, nat_qkv, gq, gk, gv, la) = _in_proj(
            x2, s, tabs, norm_g[l][None, :], w_pad, mla_q_norm_g[l][None, :], uq,
            mla_kv_norm_g[l][None, :], ukv, wgate, bgate, tm)
        sh = lambda t: t.reshape(b, s, t.shape[-1])
        o_a = _mla(sh(q), sh(kc), sh(va), tq)
        obs, lses = [], []
        for dil_qkv, (window, _) in zip((d0, d1, d2), DIL_PATTERNS):
            o_g, lse_g = _dilated_group(dil_qkv, window)
            obs.append(o_g)
            lses.append(lse_g)
        o_c = _nat(sh(nat_qkv), nat_rpb[l])
        o_d = _gla(sh(gq), sh(gk), sh(gv), sh(la), gla_norm_g[l].reshape(1, W_D))
        w_p = jnp.concatenate([w_proj_a[l], w_proj_b[l], w_proj_c[l], w_proj_d[l]], axis=0)
        x2 = _merge(x2, s, norm_g[l][None, :], o_a.reshape(b * s, W_A), obs, lses,
                    o_c.reshape(b * s, W_C), o_d.reshape(b * s, W_D), w_z,
                    w_merge[l].astype(BF16), b_merge[l][None, :], w_p.astype(BF16),
                    w_out[l].astype(BF16), final_norm_g[None, :], l == depth - 1, tm)
    return x2.reshape(b, s, d)
```

```python
import functools

import numpy as np
import jax
import jax.numpy as jnp
from jax import lax
from jax.experimental import pallas as pl
from jax.experimental.pallas import tpu as pltpu

F32 = jnp.float32
BF16 = jnp.bfloat16

GRID_W = 64
ROPE_THETA = 10000.0
NORM_EPS = 1e-6
NEG_INF = -1e30

MLA_HEADS = 4
MLA_NOPE = 128
MLA_ROPE = 64
MLA_V = 128
MLA_Q_RANK = 256
MLA_KV_RANK = 128
DIL_PATTERNS = ((128, 1), (512, 4), (2048, 16))
DIL_HEADS = 4
DIL_DH = 64
NAT_HEADS = 8
NAT_DH = 64
NAT_KH_MAX = 8
NAT_KW = 16
GLA_HEADS = 4
GLA_DK = 64
GLA_DV = 128
GLA_GATE_RANK = 16
GLA_TAU = 16.0
GLA_CHUNK = 64

W_A = MLA_HEADS * MLA_V
W_B = DIL_HEADS * DIL_DH
W_C = NAT_HEADS * NAT_DH
W_D = GLA_HEADS * GLA_DV
N_BRANCH = 4
W_DIL_GROUP = 3 * W_B
W_GLA_QK = GLA_HEADS * GLA_DK

LOG2_E = 1.4426950408889634
LANES = 128
MLA_HEAD_STRIDE = 256
MLA_SUB_QUERIES = 512


def _largest_divisor(n, candidates):
    return next((c for c in candidates if n % c == 0), 1)
VMEM_LIMIT = 56 * 1024 * 1024

_IN_SIZES = (MLA_Q_RANK, MLA_KV_RANK, MLA_ROPE, 3 * W_DIL_GROUP, W_C, W_C, W_C,
             W_GLA_QK, W_GLA_QK, W_D, GLA_GATE_RANK, GLA_GATE_RANK, W_A, W_B, W_C, W_D)
_IN_OFF = tuple(int(v) for v in np.concatenate([[0], np.cumsum(_IN_SIZES)]))

_P_Q = 0
_P_KV = _P_Q + MLA_Q_RANK
_P_KR = _P_KV + MLA_KV_RANK
_P_DIL = _P_KR + LANES
_P_NAT = _P_DIL + 3 * W_DIL_GROUP
_P_GQ = _P_NAT + 3 * W_C
_P_GK = _P_GQ + W_GLA_QK
_P_GV = _P_GK + W_GLA_QK
_P_GG = _P_GV + W_D
_P_END = _P_GG + LANES


def _dot(a, b):
    return jnp.dot(a, b, preferred_element_type=F32)


def _dot_nt(a, b):
    return lax.dot_general(a, b, (((1,), (1,)), ((), ())), preferred_element_type=F32)


def _rms(x, g):
    return x * lax.rsqrt(jnp.mean(x * x, axis=-1, keepdims=True) + NORM_EPS) * g


def _rope_cols(w):
    half = MLA_ROPE // 2
    t = w.reshape(w.shape[:-1] + (w.shape[-1] // LANES, 2, 2, half))
    return jnp.swapaxes(t, -3, -2).reshape(w.shape)


def _rope128(x, cos, sin_signed):
    return x * cos + pltpu.roll(x, LANES // 2, 1) * sin_signed


def _sigmoid(x):
    return 1.0 / (1.0 + jnp.exp(-x))


def _log_sigmoid(x):
    return jnp.minimum(x, 0.0) - jnp.log1p(jnp.exp(-jnp.abs(x)))


def _params(*sem):
    return pltpu.CompilerParams(dimension_semantics=sem, vmem_limit_bytes=VMEM_LIMIT)


def _const_spec(shape):
    nd = len(shape)
    return pl.BlockSpec(shape, lambda *_: (0,) * nd, pipeline_mode=pl.Buffered(1))


_PERM_DILATIONS = tuple(r for _, r in DIL_PATTERNS if r > 1)


def _perm_slot(r):
    return _PERM_DILATIONS.index(r)


PERM_BLOCK = 256
IN_PROJ_NORM_ROWS = 64
IN_PROJ_LAG = 6
MERGE_GATE_COLS = 512
MERGE_LAG = 2


def _perm_tables(inverse):
    tabs = []
    for r in _PERM_DILATIONS:
        p = np.zeros((PERM_BLOCK, PERM_BLOCK), np.float32)
        tok = np.arange(PERM_BLOCK)
        p[(tok % r) * (PERM_BLOCK // r) + tok // r, tok] = 1.0
        tabs.append(p.T if inverse else p)
    return jnp.asarray(np.stack(tabs), BF16)


def _interleave(gens, lag):
    gens = list(gens)
    alive = [True] * len(gens)
    step = 0
    while any(alive):
        for i, gen in enumerate(gens):
            if alive[i] and step >= i * lag:
                try:
                    next(gen)
                except StopIteration:
                    alive[i] = False
        step += 1


def _in_proj_kernel(x_ref, cos_ref, sin_ref, g_ref, w_ref, qg_ref, wuq_ref, kvg_ref,
                    wukv_ref, wgate_ref, bgate_ref, perm_ref,
                    q_ref, kc_ref, va_ref, d0_ref, d1_ref, d2_ref, nat_ref, gq_ref, gk_ref,
                    gv_ref, la_ref, y_ref):
    blocks = [_in_proj_block(blk, x_ref, cos_ref, sin_ref, g_ref, w_ref, qg_ref, wuq_ref,
                             kvg_ref, wukv_ref, wgate_ref, bgate_ref, perm_ref, q_ref, kc_ref,
                             va_ref, (d0_ref, d1_ref, d2_ref), nat_ref, gq_ref, gk_ref, gv_ref,
                             la_ref, y_ref)
              for blk in range(x_ref.shape[0] // PERM_BLOCK)]
    _interleave(blocks, lag=IN_PROJ_LAG)


def _in_proj_block(blk, x_ref, cos_ref, sin_ref, g_ref, w_ref, qg_ref, wuq_ref, kvg_ref,
                   wukv_ref, wgate_ref, bgate_ref, perm_ref, q_ref, kc_ref, va_ref, d_refs,
                   nat_ref, gq_ref, gk_ref, gv_ref, la_ref, y_ref):
    r0 = blk * PERM_BLOCK
    rows = slice(r0, r0 + PERM_BLOCK)
    chunks = []
    for c0 in range(r0, r0 + PERM_BLOCK, IN_PROJ_NORM_ROWS):
        chunks.append(_rms(x_ref[c0:c0 + IN_PROJ_NORM_ROWS, :], g_ref[...]).astype(BF16))
        yield
    hb = jnp.concatenate(chunks, axis=0)
    cos, sin = cos_ref[rows, :], sin_ref[rows, :]

    def mm(a, width):
        return _dot(hb, w_ref[:, a:a + width])

    def rope(t):
        return _rope128(t, cos, sin)

    qn = _rms(mm(_P_Q, MLA_Q_RANK), qg_ref[...]).astype(BF16)
    yield
    for h in range(MLA_HEADS):
        base = h * MLA_HEAD_STRIDE
        qh = _dot(qn, wuq_ref[:, base:base + MLA_HEAD_STRIDE])
        q_ref[rows, base:base + LANES] = qh[:, :LANES].astype(BF16)
        q_ref[rows, base + LANES:base + 2 * LANES] = rope(qh[:, LANES:]).astype(BF16)
        yield
    kv_kr = mm(_P_KV, MLA_KV_RANK + LANES)
    kvn = _rms(kv_kr[:, :MLA_KV_RANK], kvg_ref[...]).astype(BF16)
    kpe = rope(kv_kr[:, MLA_KV_RANK:]).astype(BF16)
    yield
    k_nope = _dot(kvn, wukv_ref[:, :W_A])
    for h in range(MLA_HEADS):
        base = h * MLA_HEAD_STRIDE
        kc_ref[rows, base:base + LANES] = k_nope[:, h * LANES:(h + 1) * LANES].astype(BF16)
        kc_ref[rows, base + LANES:base + 2 * LANES] = kpe
    yield
    va_ref[rows, :] = _dot(kvn, wukv_ref[:, W_A:2 * W_A]).astype(BF16)
    yield

    for g, (d_ref, (_, r)) in enumerate(zip(d_refs, DIL_PATTERNS)):
        base = _P_DIL + g * W_DIL_GROUP
        qk = mm(base, 2 * W_B)
        tiles = [(rope(qk[:, c:c + LANES]) * (DIL_DH ** -0.5 if c < W_B else 1.0)).astype(BF16)
                 for c in range(0, 2 * W_B, LANES)]
        if r == 1:
            for t, c in zip(tiles, range(0, 2 * W_B, LANES)):
                d_ref[0, 0, rows, c:c + LANES] = t
            yield
            d_ref[0, 0, rows, 2 * W_B:] = mm(base + 2 * W_B, W_B).astype(BF16)
            yield
            continue
        for t, c in zip(tiles, range(0, 2 * W_B, LANES)):
            y_ref[g, blk, :, c:c + LANES] = t
        yield
        y_ref[g, blk, :, 2 * W_B:] = mm(base + 2 * W_B, W_B).astype(BF16)
        per = PERM_BLOCK // r
        y = _dot(perm_ref[_perm_slot(r)], y_ref[g, blk]).astype(BF16)
        d_ref[0, :, blk * per:(blk + 1) * per, :] = y.reshape(r, per, W_DIL_GROUP)
        yield

    nat_ref[rows, :W_C] = (mm(_P_NAT, W_C) * (NAT_DH ** -0.5)).astype(BF16)
    yield
    for c in range(W_C, 3 * W_C, W_C):
        nat_ref[rows, c:c + W_C] = mm(_P_NAT + c, W_C).astype(BF16)
        yield

    gq_ref[rows, :] = mm(_P_GQ, W_GLA_QK) * (GLA_DK ** -0.5)
    yield
    gk_ref[rows, :] = mm(_P_GK, W_GLA_QK)
    yield
    gv_ref[rows, :] = mm(_P_GV, W_D).astype(BF16)
    yield
    gates = _dot(mm(_P_GG, LANES).astype(BF16), wgate_ref[...]) + bgate_ref[...]
    for c in range(0, 2 * W_GLA_QK, LANES):
        yield
        la_ref[rows, c:c + LANES] = _log_sigmoid(gates[:, c:c + LANES]) * (1.0 / GLA_TAU)


def _in_proj(x2, seq, tabs, g, w, qg, wuq, kvg, wukv, wgate, bgate, tm):
    n, d = x2.shape
    nseq = seq // tm
    batch = n // seq
    row = lambda width: pl.BlockSpec((tm, width), lambda i: (i, 0))
    tab = pl.BlockSpec((tm, LANES), lambda i: (i % nseq, 0))
    sub = lambda r, width: pl.BlockSpec((1, r, tm // r, width),
                                        lambda i: (i // nseq, 0, i % nseq, 0))
    sub_shape = lambda r, width, dt: jax.ShapeDtypeStruct((batch, r, seq // r, width), dt)
    dil = [r for _, r in DIL_PATTERNS]
    outs = [(4 * MLA_HEAD_STRIDE, BF16), (4 * MLA_HEAD_STRIDE, BF16), (W_A, BF16)]
    outs2 = [(3 * W_C, BF16), (W_GLA_QK, F32), (W_GLA_QK, F32), (W_D, BF16), (2 * W_GLA_QK, F32)]
    assert tm % PERM_BLOCK == 0
    perm = _perm_tables(False)
    consts = (g, w, qg, wuq, kvg, wukv, wgate, bgate, perm)
    return pl.pallas_call(
        _in_proj_kernel,
        grid=(n // tm,),
        in_specs=[row(d), tab, tab] + [_const_spec(a.shape) for a in consts],
        out_specs=([row(wd) for wd, _ in outs] + [sub(r, W_DIL_GROUP) for r in dil]
                   + [row(wd) for wd, _ in outs2]),
        out_shape=([jax.ShapeDtypeStruct((n, wd), dt) for wd, dt in outs]
                   + [sub_shape(r, W_DIL_GROUP, BF16) for r in dil]
                   + [jax.ShapeDtypeStruct((n, wd), dt) for wd, dt in outs2]),
        scratch_shapes=[pltpu.VMEM((len(dil), tm // PERM_BLOCK, PERM_BLOCK, W_DIL_GROUP), BF16)],
        compiler_params=_params("parallel"),
        name="in_proj",
    )(x2, *tabs, *consts)


def _mla_kernel(q_ref, k_ref, v_ref, o_ref, vx_ref):
    @pl.when(pl.program_id(2) == 0)
    def _():
        vx_ref[:, :MLA_V] = v_ref[0]
        vx_ref[:, MLA_V:] = jnp.ones((vx_ref.shape[0], MLA_V), BF16)

    c = (MLA_NOPE + MLA_ROPE) ** -0.5 * LOG2_E
    sub = MLA_SUB_QUERIES
    nsub = q_ref.shape[1] // sub
    scores = [_dot_nt(q_ref[0, i * sub:(i + 1) * sub, :], k_ref[0]) for i in range(nsub)]
    for i, s in enumerate(scores):
        m = jnp.max(s, axis=-1, keepdims=True)
        p = jnp.exp2((s - m) * c).astype(BF16)
        ov = _dot(p, vx_ref[...])
        o_ref[0, i * sub:(i + 1) * sub, :] = (ov[:, :MLA_V] / ov[:, MLA_V:]).astype(o_ref.dtype)


def _mla(q, kc, va, tq):
    b, s, _ = q.shape
    return pl.pallas_call(
        _mla_kernel,
        grid=(b, MLA_HEADS, s // tq),
        in_specs=[pl.BlockSpec((1, tq, MLA_HEAD_STRIDE), lambda i, h, j: (i, j, h)),
                  pl.BlockSpec((1, s, MLA_HEAD_STRIDE), lambda i, h, j: (i, 0, h)),
                  pl.BlockSpec((1, s, MLA_V), lambda i, h, j: (i, 0, h))],
        out_specs=pl.BlockSpec((1, tq, MLA_V), lambda i, h, j: (i, j, h)),
        out_shape=jax.ShapeDtypeStruct((b, s, W_A), BF16),
        scratch_shapes=[pltpu.VMEM((s, 2 * MLA_V), BF16)],
        compiler_params=_params("parallel", "parallel", "arbitrary"),
        name="mla_attention",
    )(q, kc, va)


def _stack_heads(x128, rope_order=False):
    lane = lax.broadcasted_iota(jnp.int32, x128.shape, 1)
    first = ((lane & 32) == 0) if rope_order else (lane < 64)
    zero = jnp.zeros_like(x128)
    return jnp.concatenate([jnp.where(first, x128, zero), jnp.where(first, zero, x128)], axis=0)


def _unstack_heads(y2):
    m = y2.shape[0] // 2
    lane = lax.broadcasted_iota(jnp.int32, (m, y2.shape[1]), 1)
    return jnp.where(lane < 64, y2[:m], y2[m:])


def _pair_scores(q128, k128, rope_order=False):
    return _dot_nt(_stack_heads(q128, rope_order), k128)


def _pair_softmax_pv(s, bias, v128, want_lse):
    s = s + bias
    mx = jnp.max(s, axis=-1, keepdims=True)
    p = jnp.exp(s - mx)
    den = jnp.sum(p, axis=-1, keepdims=True)
    o = _unstack_heads(_dot(p.astype(BF16), v128) / den)
    if not want_lse:
        return o, None
    lse = jnp.broadcast_to(mx + jnp.log(den), (s.shape[0], LANES))
    return o, _unstack_heads(lse)


def _dil_kernel(qkv_ref, bias_ref, o_ref, lse_ref, s_ref, *, length, qb, win, radius, group):
    per_seq = length // (qb * group)

    def body(i, carry):
        sub = i // per_seq
        jobs = []
        for u in range(group):
            q0 = pl.multiple_of(((i % per_seq) * group + u) * qb, qb)
            ks = pl.multiple_of(jnp.clip(q0 - radius, 0, length - win), 64)
            for c in range(0, W_B, LANES):
                q128 = qkv_ref[sub, pl.ds(q0, qb), c:c + LANES]
                k128 = qkv_ref[sub, pl.ds(ks, win), W_B + c:W_B + c + LANES]
                s_ref[len(jobs)] = _pair_scores(q128, k128, rope_order=True)
                jobs.append((q0, ks, c))
        for slot, (q0, ks, c) in enumerate(jobs):
            v128 = qkv_ref[sub, pl.ds(ks, win), 2 * W_B + c:2 * W_B + c + LANES]
            o, lse = _pair_softmax_pv(s_ref[slot], bias_ref[(q0 - ks) // 64], v128, True)
            o_ref[sub, pl.ds(q0, qb), c:c + LANES] = o.astype(o_ref.dtype)
            lse_ref[sub, pl.ds(q0, qb), c:c + LANES] = lse
        return carry

    lax.fori_loop(0, qkv_ref.shape[0] * per_seq, body, 0)


def _dil_bias_table(qb, win, radius):
    nvar = (win - qb) // 64 + 1
    i = (np.arange(2 * qb) % qb)[None, :, None]
    j = np.arange(win)[None, None, :]
    delta = (64 * np.arange(nvar))[:, None, None]
    return jnp.asarray(np.where(np.abs(i + delta - j) <= radius, 0.0, NEG_INF), F32)


def _dilated_group(qkv, window):
    b, r, length, _ = qkv.shape
    radius = window // (2 * r)
    qb = min(128, length)
    win = min(qb + 2 * radius, length)
    assert length % qb == 0 and radius % 64 == 0 and qb % 64 == 0
    table = _dil_bias_table(qb, win, radius)
    group = _largest_divisor(length // qb, (4, 2))
    nsub = max(1, r // 4)
    sub = lambda width: pl.BlockSpec((None, nsub, length, width), lambda i, m: (i, m, 0, 0))
    return pl.pallas_call(
        functools.partial(_dil_kernel, length=length, qb=qb, win=win, radius=radius, group=group),
        grid=(b, r // nsub),
        in_specs=[sub(W_DIL_GROUP), _const_spec(table.shape)],
        out_specs=[sub(W_B), sub(W_B)],
        out_shape=[jax.ShapeDtypeStruct((b, r, length, W_B), BF16),
                   jax.ShapeDtypeStruct((b, r, length, W_B), F32)],
        scratch_shapes=[pltpu.VMEM((group * DIL_HEADS // 2, 2 * qb, win), F32)],
        compiler_params=_params("parallel", "parallel"),
        name=f"dilated_attention_r{r}",
    )(qkv, table)


def _nat_bias_table(rpb):
    qc = np.arange(GRID_W)[:, None]
    kc = np.arange(GRID_W)[None, :]
    win_start = np.clip(qc - NAT_KW // 2, 0, GRID_W - NAT_KW)
    allowed = (kc >= win_start) & (kc < win_start + NAT_KW)
    n_row = 2 * NAT_KH_MAX - 1
    side = GRID_W - NAT_KW
    g = jnp.pad(rpb, ((0, 0), (0, 0), (side, side + 1)))
    a = jnp.broadcast_to(g[:, :, None, :], (NAT_HEADS, n_row, GRID_W, 2 * GRID_W))
    a = a.reshape(NAT_HEADS, n_row, 2 * GRID_W * GRID_W)
    a = a[:, :, GRID_W - 1:GRID_W - 1 + GRID_W * (2 * GRID_W - 1)]
    toep = a.reshape(NAT_HEADS, n_row, GRID_W, 2 * GRID_W - 1)[..., :GRID_W]
    toep = jnp.where(allowed[None, None], toep, NEG_INF).astype(F32)
    return jnp.concatenate([toep[:, :-1], toep[:, 1:]], axis=-1)


def _nat_kernel(q_ref, k_ref, v_ref, bias_ref, o_ref, s_ref, *, rows, group):
    kh = NAT_KH_MAX

    def body(i, carry):
        jobs = []
        for u in range(group):
            r = i * group + u
            rs = jnp.clip(r - kh // 2, 0, rows - kh)
            q0 = pl.multiple_of(r * GRID_W, GRID_W)
            k0 = pl.multiple_of(rs * GRID_W, GRID_W)
            s_ref[u] = _pair_scores(q_ref[0, pl.ds(q0, GRID_W), :],
                                    k_ref[0, pl.ds(k0, kh * GRID_W), :])
            jobs.append((q0, k0, rs - r + NAT_KH_MAX - 1))
        for u, (q0, k0, var) in enumerate(jobs):
            v128 = v_ref[0, pl.ds(k0, kh * GRID_W), :]
            bias = jnp.concatenate(
                [jnp.concatenate([bias_ref[hh, var + j] for j in range(0, kh, 2)], axis=1)
                 for hh in range(2)], axis=0)
            o, _ = _pair_softmax_pv(s_ref[u], bias, v128, False)
            o_ref[0, pl.ds(q0, GRID_W), :] = o.astype(o_ref.dtype)
        return carry

    lax.fori_loop(0, rows // group, body, 0)


def _nat(qkv, rpb):
    b, s, _ = qkv.shape
    rows = s // GRID_W
    assert rows >= NAT_KH_MAX and NAT_KH_MAX % 2 == 0
    table = _nat_bias_table(rpb)
    npair = NAT_HEADS // 2
    col = lambda off: pl.BlockSpec((1, s, LANES), lambda i, p: (i, 0, off + p))
    group = _largest_divisor(rows, (16, 8, 4, 2))
    return pl.pallas_call(
        functools.partial(_nat_kernel, rows=rows, group=group),
        grid=(b, npair),
        in_specs=[col(0), col(npair), col(2 * npair),
                  pl.BlockSpec((2,) + table.shape[1:], lambda i, p: (p, 0, 0, 0))],
        out_specs=pl.BlockSpec((1, s, LANES), lambda i, p: (i, 0, p)),
        out_shape=jax.ShapeDtypeStruct((b, s, W_C), BF16),
        scratch_shapes=[pltpu.VMEM((group, 2 * GRID_W, NAT_KH_MAX * GRID_W), F32)],
        compiler_params=_params("parallel", "parallel"),
        name="neighbourhood_attention",
    )(qkv, qkv, qkv, table)


def _split3(x):
    hi = x.astype(BF16)
    r1 = x - hi.astype(F32)
    mid = r1.astype(BF16)
    lo = (r1 - mid.astype(F32)).astype(BF16)
    return hi, mid, lo


def _gla_kernel(q_ref, k_ref, v_ref, laf_ref, lab_ref, g_ref, o_ref, accf_ref, accb_ref, st_ref,
                *, seq):
    c = GLA_CHUNK
    nchunk = seq // c
    ri = lax.broadcasted_iota(jnp.int32, (c, c), 0)
    ci = lax.broadcasted_iota(jnp.int32, (c, c), 1)
    r2 = lax.broadcasted_iota(jnp.int32, (2 * c, 2 * c), 0)
    c2 = lax.broadcasted_iota(jnp.int32, (2 * c, 2 * c), 1)
    same_head = (r2 >= c) == (c2 >= c)
    zero_v = jnp.zeros((c, GLA_DV), BF16)
    st_ref[...] = jnp.zeros_like(st_ref)

    dirs = ((laf_ref, accf_ref, ci <= ri, same_head & (c2 <= r2), c - 1, c // 2 - 1),
            (lab_ref, accb_ref, ci >= ri, same_head & (c2 >= r2), 0, c // 2))
    group = _largest_divisor(nchunk, (8, 4, 2))

    def body(i, carry):
        jobs = []
        for d, (la_ref, acc_ref, tri, keep, total_row, mid_row) in enumerate(dirs):
            for u in range(group):
                n = i * group + u
                r0 = pl.multiple_of((nchunk - 1 - n if d else n) * c, c)
                hi, mid, lo = _split3(la_ref[0, pl.ds(r0, c), :])
                tri_b = tri.astype(BF16)
                bcum = _dot(tri_b, hi) + _dot(tri_b, mid) + _dot(tri_b, lo)
                jobs.append(dict(d=d, r0=r0, bcum=bcum, keep=keep, acc=acc_ref,
                                 b_tot=bcum[total_row:total_row + 1, :],
                                 b_mid=bcum[mid_row:mid_row + 1, :]))
        for j in jobs:
            q = q_ref[0, pl.ds(j["r0"], c), :]
            k = k_ref[0, pl.ds(j["r0"], c), :]
            v = v_ref[0, pl.ds(j["r0"], c), :]
            q_in = _stack_heads(q * jnp.exp(j["bcum"] - j["b_mid"])).astype(BF16)
            k_in = _stack_heads(k * jnp.exp(j["b_mid"] - j["bcum"])).astype(BF16)
            k_dec = (k * jnp.exp(j["b_tot"] - j["bcum"])).astype(BF16)
            j["q_st"] = _stack_heads(q * jnp.exp(j["bcum"])).astype(BF16)
            j["kv_t"] = _dot(v.astype(F32).T.astype(BF16), k_dec)
            j["att"] = _dot_nt(q_in, k_in)
            j["v2"] = jnp.concatenate([jnp.concatenate([v[:, :GLA_DV], zero_v], axis=1),
                                       jnp.concatenate([zero_v, v[:, GLA_DV:]], axis=1)], axis=0)
        states = [st_ref[0], st_ref[1]]
        for j in jobs:
            state = states[j["d"]]
            j["inter"] = _dot_nt(j["q_st"], state.astype(BF16))
            states[j["d"]] = state * jnp.exp(j["b_tot"]) + j["kv_t"]
        st_ref[0], st_ref[1] = states
        for j in jobs:
            att = jnp.where(j["keep"], j["att"], 0.0).astype(BF16)
            tot = _dot(att, j["v2"]) + j["inter"]
            j["acc"][pl.ds(j["r0"], c), :] = jnp.concatenate(
                [tot[:c, :GLA_DV], tot[c:, GLA_DV:]], axis=1)
        return carry

    lax.fori_loop(0, nchunk // group, body, 0)
    for hh in range(2):
        sl = slice(hh * GLA_DV, (hh + 1) * GLA_DV)
        o_ref[0, :, sl] = _rms(accf_ref[:, sl] + accb_ref[:, sl], g_ref[:, sl]).astype(o_ref.dtype)


def _gla(gq, gk, gv, la, gain):
    b, s, _ = gq.shape
    npair = GLA_HEADS // 2
    qk = pl.BlockSpec((1, s, LANES), lambda i, p: (i, 0, p))
    return pl.pallas_call(
        functools.partial(_gla_kernel, seq=s),
        grid=(b, npair),
        in_specs=[qk, qk,
                  pl.BlockSpec((1, s, 2 * GLA_DV), lambda i, p: (i, 0, p)),
                  pl.BlockSpec((1, s, LANES), lambda i, p: (i, 0, p)),
                  pl.BlockSpec((1, s, LANES), lambda i, p: (i, 0, npair + p)),
                  pl.BlockSpec((1, 2 * GLA_DV), lambda i, p: (0, p))],
        out_specs=pl.BlockSpec((1, s, 2 * GLA_DV), lambda i, p: (i, 0, p)),
        out_shape=jax.ShapeDtypeStruct((b, s, W_D), BF16),
        scratch_shapes=[pltpu.VMEM((s, 2 * GLA_DV), F32), pltpu.VMEM((s, 2 * GLA_DV), F32),
                        pltpu.VMEM((2, 2 * GLA_DV, LANES), F32)],
        compiler_params=_params("parallel", "parallel"),
        name="gated_linear_attention",
    )(gq, gk, gv, la, la, gain)


def _merge_kernel(x_ref, g_ref, oa_ref, ob0_ref, ob1_ref, ob2_ref, l0_ref, l1_ref, l2_ref,
                  oc_ref, od_ref, wz_ref, wm_ref, bm_ref, wp_ref, wo_ref, fg_ref, perm_ref, y_ref,
                  *, final):
    blocks = [_merge_block(blk, x_ref, g_ref, oa_ref, (ob0_ref, ob1_ref, ob2_ref),
                           (l0_ref, l1_ref, l2_ref), oc_ref, od_ref, wz_ref, wm_ref, bm_ref,
                           wp_ref, wo_ref, fg_ref, perm_ref, y_ref, final)
              for blk in range(x_ref.shape[0] // PERM_BLOCK)]
    _interleave(blocks, lag=MERGE_LAG)


def _merge_block(blk, x_ref, g_ref, oa_ref, ob_refs, l_refs, oc_ref, od_ref, wz_ref, wm_ref,
                 bm_ref, wp_ref, wo_ref, fg_ref, perm_ref, y_ref, final):
    rows = slice(blk * PERM_BLOCK, (blk + 1) * PERM_BLOCK)
    x = x_ref[rows, :]
    d = x.shape[1]
    hb = _rms(x, g_ref[...]).astype(BF16)
    yield

    obs, lses = [], []
    for (_, r), o_ref, l_ref in zip(DIL_PATTERNS, ob_refs, l_refs):
        if r == 1:
            obs.append(o_ref[0, 0, rows, :].astype(F32))
            lses.append(l_ref[0, 0, rows, :])
            continue
        pinv = perm_ref[_perm_slot(r)]
        per = PERM_BLOCK // r
        sub_rows = slice(blk * per, (blk + 1) * per)
        obs.append(_dot(pinv, o_ref[0, :, sub_rows, :].reshape(PERM_BLOCK, W_B)))
        parts = _split3(l_ref[0, :, sub_rows, :].reshape(PERM_BLOCK, W_B))
        lses.append(sum(_dot(pinv, part) for part in parts))
        yield

    mx = jnp.maximum(jnp.maximum(lses[0], lses[1]), lses[2])
    es = [jnp.exp(l - mx) for l in lses]
    tot = es[0] + es[1] + es[2]
    o_b = sum(e / tot * o for e, o in zip(es, obs))
    yield

    branches = ((oa_ref, W_A), (None, W_B), (oc_ref, W_C), (od_ref, W_D))
    mixed = [jnp.zeros((PERM_BLOCK, MERGE_GATE_COLS), F32) for _ in range(d // MERGE_GATE_COLS)]
    off = 0
    for i, (o_ref, width) in enumerate(branches):
        o = o_b if o_ref is None else o_ref[rows, :].astype(F32)
        z = _dot(hb, wz_ref[:, off:off + width])
        u = (o * (z * _sigmoid(z))).astype(BF16)
        yield
        proj = _dot(u, wp_ref[off:off + width, :])
        yield
        for n, c in enumerate(range(0, d, MERGE_GATE_COLS)):
            cols = slice(i * d + c, i * d + c + MERGE_GATE_COLS)
            gate = _sigmoid(_dot(hb, wm_ref[:, cols]) + bm_ref[:, cols])
            mixed[n] = mixed[n] + gate * proj[:, c:c + MERGE_GATE_COLS]
            yield
        off += width
    y = x + _dot(jnp.concatenate(mixed, axis=1).astype(BF16), wo_ref[...])
    if final:
        y = _rms(y, fg_ref[...])
    y_ref[rows, :] = y


def _merge(x2, seq, g, oa, obs, lses, oc, od, wz, wm, bm, wp, wo, fg, final, tm):
    n, d = x2.shape
    nseq = seq // tm
    row = lambda width: pl.BlockSpec((tm, width), lambda i: (i, 0))
    sub = lambda t: pl.BlockSpec((1, t.shape[1], tm // t.shape[1], W_B),
                                 lambda i: (i // nseq, 0, i % nseq, 0))
    consts = (wz, wm, bm, wp, wo, fg, _perm_tables(True))
    return pl.pallas_call(
        functools.partial(_merge_kernel, final=final),
        grid=(n // tm,),
        in_specs=[row(d), _const_spec(g.shape), row(W_A)] + [sub(t) for t in (*obs, *lses)]
                 + [row(W_C), row(W_D)] + [_const_spec(a.shape) for a in consts],
        out_specs=row(d),
        out_shape=jax.ShapeDtypeStruct((n, d), F32),
        compiler_params=_params("parallel"),
        name="gate_merge_out",
    )(x2, g, oa, *obs, *lses, oc, od, *consts)


def _rope_tables(seq):
    half = MLA_ROPE // 2
    inv = jnp.power(ROPE_THETA, -jnp.arange(0, MLA_ROPE, 2, dtype=F32) / MLA_ROPE)
    ang = jnp.arange(seq, dtype=jnp.int32).astype(F32)[:, None] * inv[None, :]
    cos, sin = jnp.cos(ang), jnp.sin(ang)
    return jnp.concatenate([cos] * 4, axis=-1), jnp.concatenate([-sin, -sin, sin, sin], axis=-1)


def _prep_layer(w_in, w_uq, w_ukv, w_gf, b_gf, w_gb, b_gb):
    d = w_in.shape[0]
    o = _IN_OFF
    zpad = lambda width: jnp.zeros((d, width), w_in.dtype)
    dil = w_in[:, o[3]:o[4]].reshape(d, len(DIL_PATTERNS), 3, W_B)
    dil = jnp.concatenate([_rope_cols(dil[:, :, :2]), dil[:, :, 2:]], axis=2)
    w_pad = jnp.concatenate([
        w_in[:, o[0]:o[2]],
        _rope_cols(jnp.concatenate([w_in[:, o[2]:o[3]], zpad(LANES - MLA_ROPE)], axis=1)),
        dil.reshape(d, 3 * W_DIL_GROUP),
        w_in[:, o[4]:o[10]],
        w_in[:, o[10]:o[12]], zpad(LANES - 2 * GLA_GATE_RANK)], axis=1)
    w_z = w_in[:, o[12]:o[16]]
    uq = w_uq.reshape(MLA_Q_RANK, MLA_HEADS, MLA_NOPE + MLA_ROPE)
    uq = jnp.pad(uq, ((0, 0), (0, 0), (0, MLA_HEAD_STRIDE - MLA_NOPE - MLA_ROPE)))
    uq = jnp.concatenate([uq[:, :, :MLA_NOPE], _rope_cols(uq[:, :, MLA_NOPE:])], axis=2)
    uq = uq.reshape(MLA_Q_RANK, MLA_HEADS * MLA_HEAD_STRIDE).astype(BF16)
    ukv = w_ukv.reshape(MLA_KV_RANK, MLA_HEADS, MLA_NOPE + MLA_V)
    ukv = jnp.concatenate([ukv[:, :, :MLA_NOPE].reshape(MLA_KV_RANK, -1),
                           ukv[:, :, MLA_NOPE:].reshape(MLA_KV_RANK, -1)], axis=1).astype(BF16)
    wgate = jnp.zeros((LANES, 2 * W_GLA_QK), F32)
    wgate = wgate.at[:GLA_GATE_RANK, :W_GLA_QK].set(w_gf)
    wgate = wgate.at[GLA_GATE_RANK:2 * GLA_GATE_RANK, W_GLA_QK:].set(w_gb)
    bgate = jnp.concatenate([b_gf, b_gb])[None, :]
    return w_pad, w_z, uq, ukv, wgate.astype(BF16), bgate


def kernel(x, norm_g, w_in, mla_q_norm_g, mla_w_uq, mla_kv_norm_g, mla_w_ukv, nat_rpb,
           gla_w_gate_f, gla_b_gate_f, gla_w_gate_b, gla_b_gate_b, gla_norm_g,
           w_proj_a, w_proj_b, w_proj_c, w_proj_d, w_merge, b_merge, w_out, final_norm_g):
    b, s, d = x.shape
    depth = w_in.shape[0]
    tm = min(512, s)
    tq = min(1024, s)
    tabs = _rope_tables(s)
    x2 = x.reshape(b * s, d)
    w_in_b = w_in.astype(BF16)
    for l in range(depth):
        w_pad, w_z, uq, ukv, wgate, bgate = _prep_layer(
            w_in_b[l], mla_w_uq[l], mla_w_ukv[l], gla_w_gate_f[l], gla_b_gate_f[l],
            gla_w_gate_b[l], gla_b_gate_b[l])
        (q, kc, va, d0, d1, d2, nat_qkv, gq, gk, gv, la) = _in_proj(
            x2, s, tabs, norm_g[l][None, :], w_pad, mla_q_norm_g[l][None, :], uq,
            mla_kv_norm_g[l][None, :], ukv, wgate, bgate, tm)
        sh = lambda t: t.reshape(b, s, t.shape[-1])
        o_a = _mla(sh(q), sh(kc), sh(va), tq)
        obs, lses = [], []
        for dil_qkv, (window, _) in zip((d0, d1, d2), DIL_PATTERNS):
            o_g, lse_g = _dilated_group(dil_qkv, window)
            obs.append(o_g)
            lses.append(lse_g)
        o_c = _nat(sh(nat_qkv), nat_rpb[l])
        o_d = _gla(sh(gq), sh(gk), sh(gv), sh(la), gla_norm_g[l].reshape(1, W_D))
        w_p = jnp.concatenate([w_proj_a[l], w_proj_b[l], w_proj_c[l], w_proj_d[l]], axis=0)
        x2 = _merge(x2, s, norm_g[l][None, :], o_a.reshape(b * s, W_A), obs, lses,
                    o_c.reshape(b * s, W_C), o_d.reshape(b * s, W_D), w_z,
                    w_merge[l].astype(BF16), b_merge[l][None, :], w_p.astype(BF16),
                    w_out[l].astype(BF16), final_norm_g[None, :], l == depth - 1, tm)
    return x2.reshape(b, s, d)
```

```python
import functools

import numpy as np
import jax
import jax.numpy as jnp
from jax import lax
from jax.experimental import pallas as pl
from jax.experimental.pallas import tpu as pltpu

F32 = jnp.float32
BF16 = jnp.bfloat16

GRID_W = 64
ROPE_THETA = 10000.0
NORM_EPS = 1e-6
NEG_INF = -1e30

MLA_HEADS = 4
MLA_NOPE = 128
MLA_ROPE = 64
MLA_V = 128
MLA_Q_RANK = 256
MLA_KV_RANK = 128
DIL_PATTERNS = ((128, 1), (512, 4), (2048, 16))
DIL_HEADS = 4
DIL_DH = 64
NAT_HEADS = 8
NAT_DH = 64
NAT_KH_MAX = 8
NAT_KW = 16
GLA_HEADS = 4
GLA_DK = 64
GLA_DV = 128
GLA_GATE_RANK = 16
GLA_TAU = 16.0
GLA_CHUNK = 64

W_A = MLA_HEADS * MLA_V
W_B = DIL_HEADS * DIL_DH
W_C = NAT_HEADS * NAT_DH
W_D = GLA_HEADS * GLA_DV
N_BRANCH = 4
W_DIL_GROUP = 3 * W_B
W_GLA_QK = GLA_HEADS * GLA_DK

LOG2_E = 1.4426950408889634
LANES = 128
MLA_HEAD_STRIDE = 256
MLA_SUB_QUERIES = 512


def _largest_divisor(n, candidates):
    return next((c for c in candidates if n % c == 0), 1)
VMEM_LIMIT = 56 * 1024 * 1024

_IN_SIZES = (MLA_Q_RANK, MLA_KV_RANK, MLA_ROPE, 3 * W_DIL_GROUP, W_C, W_C, W_C,
             W_GLA_QK, W_GLA_QK, W_D, GLA_GATE_RANK, GLA_GATE_RANK, W_A, W_B, W_C, W_D)
_IN_OFF = tuple(int(v) for v in np.concatenate([[0], np.cumsum(_IN_SIZES)]))

_P_Q = 0
_P_KV = _P_Q + MLA_Q_RANK
_P_KR = _P_KV + MLA_KV_RANK
_P_DIL = _P_KR + LANES
_P_NAT = _P_DIL + 3 * W_DIL_GROUP
_P_GQ = _P_NAT + 3 * W_C
_P_GK = _P_GQ + W_GLA_QK
_P_GV = _P_GK + W_GLA_QK
_P_GG = _P_GV + W_D
_P_END = _P_GG + LANES
_PIECE_STARTS = (_P_Q, _P_DIL, _P_NAT, _P_GG)


def _dot(a, b):
    return jnp.dot(a, b, preferred_element_type=F32)


def _dot_nt(a, b):
    return lax.dot_general(a, b, (((1,), (1,)), ((), ())), preferred_element_type=F32)


def _rms(x, g):
    return x * lax.rsqrt(jnp.mean(x * x, axis=-1, keepdims=True) + NORM_EPS) * g


def _rope_cols(w):
    half = MLA_ROPE // 2
    t = w.reshape(w.shape[:-1] + (w.shape[-1] // LANES, 2, 2, half))
    return jnp.swapaxes(t, -3, -2).reshape(w.shape)


def _rope128(x, cos, sin_signed):
    return x * cos + pltpu.roll(x, LANES // 2, 1) * sin_signed


def _sigmoid(x):
    return 1.0 / (1.0 + jnp.exp(-x))


def _log_sigmoid(x):
    return jnp.minimum(x, 0.0) - jnp.log1p(jnp.exp(-jnp.abs(x)))


def _params(*sem):
    return pltpu.CompilerParams(dimension_semantics=sem, vmem_limit_bytes=VMEM_LIMIT)


def _const_spec(shape):
    nd = len(shape)
    return pl.BlockSpec(shape, lambda *_: (0,) * nd, pipeline_mode=pl.Buffered(1))


_PERM_DILATIONS = tuple(r for _, r in DIL_PATTERNS if r > 1)


def _perm_slot(r):
    return _PERM_DILATIONS.index(r)


PERM_BLOCK = 256
IN_PROJ_NORM_ROWS = 64
IN_PROJ_LAG = 6
MERGE_GATE_COLS = 512
MERGE_LAG = 2


def _perm_tables(inverse):
    tabs = []
    for r in _PERM_DILATIONS:
        p = np.zeros((PERM_BLOCK, PERM_BLOCK), np.float32)
        tok = np.arange(PERM_BLOCK)
        p[(tok % r) * (PERM_BLOCK // r) + tok // r, tok] = 1.0
        tabs.append(p.T if inverse else p)
    return jnp.asarray(np.stack(tabs), BF16)


def _interleave(gens, lag):
    gens = list(gens)
    alive = [True] * len(gens)
    step = 0
    while any(alive):
        for i, gen in enumerate(gens):
            if alive[i] and step >= i * lag:
                try:
                    next(gen)
                except StopIteration:
                    alive[i] = False
        step += 1


def _in_proj_kernel(x_ref, cos_ref, sin_ref, g_ref, w0_ref, w1_ref, w2_ref, w3_ref, qg_ref, wuq_ref, kvg_ref,
                    wukv_ref, wgate_ref, bgate_ref, perm_ref,
                    q_ref, kc_ref, va_ref, d0_ref, d1_ref, d2_ref, nat_ref, gq_ref, gk_ref,
                    gv_ref, la_ref, y_ref):
    w_refs = (w0_ref, w1_ref, w2_ref, w3_ref)
    blocks = [_in_proj_block(blk, x_ref, cos_ref, sin_ref, g_ref, w_refs, qg_ref, wuq_ref,
                             kvg_ref, wukv_ref, wgate_ref, bgate_ref, perm_ref, q_ref, kc_ref,
                             va_ref, (d0_ref, d1_ref, d2_ref), nat_ref, gq_ref, gk_ref, gv_ref,
                             la_ref, y_ref)
              for blk in range(x_ref.shape[0] // PERM_BLOCK)]
    _interleave(blocks, lag=IN_PROJ_LAG)


def _in_proj_block(blk, x_ref, cos_ref, sin_ref, g_ref, w_refs, qg_ref, wuq_ref, kvg_ref,
                   wukv_ref, wgate_ref, bgate_ref, perm_ref, q_ref, kc_ref, va_ref, d_refs,
                   nat_ref, gq_ref, gk_ref, gv_ref, la_ref, y_ref):
    r0 = blk * PERM_BLOCK
    rows = slice(r0, r0 + PERM_BLOCK)
    chunks = []
    for c0 in range(r0, r0 + PERM_BLOCK, IN_PROJ_NORM_ROWS):
        chunks.append(_rms(x_ref[c0:c0 + IN_PROJ_NORM_ROWS, :], g_ref[...]).astype(BF16))
        yield
    hb = jnp.concatenate(chunks, axis=0)
    cos, sin = cos_ref[rows, :], sin_ref[rows, :]

    def mm(a, width):
        k = max(i for i, start in enumerate(_PIECE_STARTS) if start <= a)
        off = a - _PIECE_STARTS[k]
        return _dot(hb, w_refs[k][:, off:off + width])

    def rope(t):
        return _rope128(t, cos, sin)

    qn = _rms(mm(_P_Q, MLA_Q_RANK), qg_ref[...]).astype(BF16)
    yield
    for h in range(MLA_HEADS):
        base = h * MLA_HEAD_STRIDE
        qh = _dot(qn, wuq_ref[:, base:base + MLA_HEAD_STRIDE])
        q_ref[rows, base:base + LANES] = qh[:, :LANES].astype(BF16)
        q_ref[rows, base + LANES:base + 2 * LANES] = rope(qh[:, LANES:]).astype(BF16)
        yield
    kv_kr = mm(_P_KV, MLA_KV_RANK + LANES)
    kvn = _rms(kv_kr[:, :MLA_KV_RANK], kvg_ref[...]).astype(BF16)
    kpe = rope(kv_kr[:, MLA_KV_RANK:]).astype(BF16)
    yield
    k_nope = _dot(kvn, wukv_ref[:, :W_A])
    for h in range(MLA_HEADS):
        base = h * MLA_HEAD_STRIDE
        kc_ref[rows, base:base + LANES] = k_nope[:, h * LANES:(h + 1) * LANES].astype(BF16)
        kc_ref[rows, base + LANES:base + 2 * LANES] = kpe
    yield
    va_ref[rows, :] = _dot(kvn, wukv_ref[:, W_A:2 * W_A]).astype(BF16)
    yield

    for g, (d_ref, (_, r)) in enumerate(zip(d_refs, DIL_PATTERNS)):
        base = _P_DIL + g * W_DIL_GROUP
        qk = mm(base, 2 * W_B)
        tiles = [(rope(qk[:, c:c + LANES]) * (DIL_DH ** -0.5 if c < W_B else 1.0)).astype(BF16)
                 for c in range(0, 2 * W_B, LANES)]
        if r == 1:
            for t, c in zip(tiles, range(0, 2 * W_B, LANES)):
                d_ref[0, 0, rows, c:c + LANES] = t
            yield
            d_ref[0, 0, rows, 2 * W_B:] = mm(base + 2 * W_B, W_B).astype(BF16)
            yield
            continue
        for t, c in zip(tiles, range(0, 2 * W_B, LANES)):
            y_ref[g, blk, :, c:c + LANES] = t
        yield
        y_ref[g, blk, :, 2 * W_B:] = mm(base + 2 * W_B, W_B).astype(BF16)
        per = PERM_BLOCK // r
        y = _dot(perm_ref[_perm_slot(r)], y_ref[g, blk]).astype(BF16)
        d_ref[0, :, blk * per:(blk + 1) * per, :] = y.reshape(r, per, W_DIL_GROUP)
        yield

    nat_ref[rows, :W_C] = (mm(_P_NAT, W_C) * (NAT_DH ** -0.5)).astype(BF16)
    yield
    for c in range(W_C, 3 * W_C, W_C):
        nat_ref[rows, c:c + W_C] = mm(_P_NAT + c, W_C).astype(BF16)
        yield

    gq_ref[rows, :] = mm(_P_GQ, W_GLA_QK) * (GLA_DK ** -0.5)
    yield
    gk_ref[rows, :] = mm(_P_GK, W_GLA_QK)
    yield
    gv_ref[rows, :] = mm(_P_GV, W_D).astype(BF16)
    yield
    gates = _dot(mm(_P_GG, LANES).astype(BF16), wgate_ref[...]) + bgate_ref[...]
    for c in range(0, 2 * W_GLA_QK, LANES):
        yield
        la_ref[rows, c:c + LANES] = _log_sigmoid(gates[:, c:c + LANES]) * (1.0 / GLA_TAU)


def _in_proj(x2, seq, tabs, g, w, qg, wuq, kvg, wukv, wgate, bgate, tm):
    n, d = x2.shape
    nseq = seq // tm
    batch = n // seq
    row = lambda width: pl.BlockSpec((tm, width), lambda i: (i, 0))
    tab = pl.BlockSpec((tm, LANES), lambda i: (i % nseq, 0))
    sub = lambda r, width: pl.BlockSpec((1, r, tm // r, width),
                                        lambda i: (i // nseq, 0, i % nseq, 0))
    sub_shape = lambda r, width, dt: jax.ShapeDtypeStruct((batch, r, seq // r, width), dt)
    dil = [r for _, r in DIL_PATTERNS]
    outs = [(4 * MLA_HEAD_STRIDE, BF16), (4 * MLA_HEAD_STRIDE, BF16), (W_A, BF16)]
    outs2 = [(3 * W_C, BF16), (W_GLA_QK, F32), (W_GLA_QK, F32), (W_D, BF16), (2 * W_GLA_QK, F32)]
    assert tm % PERM_BLOCK == 0
    perm = _perm_tables(False)
    consts = (g, *w, qg, wuq, kvg, wukv, wgate, bgate, perm)
    return pl.pallas_call(
        _in_proj_kernel,
        grid=(n // tm,),
        in_specs=[row(d), tab, tab] + [_const_spec(a.shape) for a in consts],
        out_specs=([row(wd) for wd, _ in outs] + [sub(r, W_DIL_GROUP) for r in dil]
                   + [row(wd) for wd, _ in outs2]),
        out_shape=([jax.ShapeDtypeStruct((n, wd), dt) for wd, dt in outs]
                   + [sub_shape(r, W_DIL_GROUP, BF16) for r in dil]
                   + [jax.ShapeDtypeStruct((n, wd), dt) for wd, dt in outs2]),
        scratch_shapes=[pltpu.VMEM((len(dil), tm // PERM_BLOCK, PERM_BLOCK, W_DIL_GROUP), BF16)],
        compiler_params=_params("parallel"),
        name="in_proj",
    )(x2, *tabs, *consts)


def _mla_kernel(q_ref, k_ref, v_ref, o_ref, vx_ref):
    @pl.when(pl.program_id(2) == 0)
    def _():
        vx_ref[:, :MLA_V] = v_ref[0]
        vx_ref[:, MLA_V:] = jnp.ones((vx_ref.shape[0], MLA_V), BF16)

    c = (MLA_NOPE + MLA_ROPE) ** -0.5 * LOG2_E
    sub = MLA_SUB_QUERIES
    nsub = q_ref.shape[1] // sub
    scores = [_dot_nt(q_ref[0, i * sub:(i + 1) * sub, :], k_ref[0]) for i in range(nsub)]
    for i, s in enumerate(scores):
        m = jnp.max(s, axis=-1, keepdims=True)
        p = jnp.exp2((s - m) * c).astype(BF16)
        ov = _dot(p, vx_ref[...])
        o_ref[0, i * sub:(i + 1) * sub, :] = (ov[:, :MLA_V] / ov[:, MLA_V:]).astype(o_ref.dtype)


def _mla(q, kc, va, tq):
    b, s, _ = q.shape
    return pl.pallas_call(
        _mla_kernel,
        grid=(b, MLA_HEADS, s // tq),
        in_specs=[pl.BlockSpec((1, tq, MLA_HEAD_STRIDE), lambda i, h, j: (i, j, h)),
                  pl.BlockSpec((1, s, MLA_HEAD_STRIDE), lambda i, h, j: (i, 0, h)),
                  pl.BlockSpec((1, s, MLA_V), lambda i, h, j: (i, 0, h))],
        out_specs=pl.BlockSpec((1, tq, MLA_V), lambda i, h, j: (i, j, h)),
        out_shape=jax.ShapeDtypeStruct((b, s, W_A), BF16),
        scratch_shapes=[pltpu.VMEM((s, 2 * MLA_V), BF16)],
        compiler_params=_params("parallel", "parallel", "arbitrary"),
        name="mla_attention",
    )(q, kc, va)


def _stack_heads(x128, rope_order=False):
    lane = lax.broadcasted_iota(jnp.int32, x128.shape, 1)
    first = ((lane & 32) == 0) if rope_order else (lane < 64)
    zero = jnp.zeros_like(x128)
    return jnp.concatenate([jnp.where(first, x128, zero), jnp.where(first, zero, x128)], axis=0)


def _unstack_heads(y2):
    m = y2.shape[0] // 2
    lane = lax.broadcasted_iota(jnp.int32, (m, y2.shape[1]), 1)
    return jnp.where(lane < 64, y2[:m], y2[m:])


def _pair_scores(q128, k128, rope_order=False):
    return _dot_nt(_stack_heads(q128, rope_order), k128)


def _pair_softmax_pv(s, bias, v128, want_lse):
    s = s + bias
    mx = jnp.max(s, axis=-1, keepdims=True)
    p = jnp.exp(s - mx)
    den = jnp.sum(p, axis=-1, keepdims=True)
    o = _unstack_heads(_dot(p.astype(BF16), v128) / den)
    if not want_lse:
        return o, None
    lse = jnp.broadcast_to(mx + jnp.log(den), (s.shape[0], LANES))
    return o, _unstack_heads(lse)


def _dil_kernel(qkv_ref, bias_ref, o_ref, lse_ref, s_ref, *, length, qb, win, radius, group):
    per_seq = length // (qb * group)

    def body(i, carry):
        sub = i // per_seq
        jobs = []
        for u in range(group):
            q0 = pl.multiple_of(((i % per_seq) * group + u) * qb, qb)
            ks = pl.multiple_of(jnp.clip(q0 - radius, 0, length - win), 64)
            for c in range(0, W_B, LANES):
                q128 = qkv_ref[sub, pl.ds(q0, qb), c:c + LANES]
                k128 = qkv_ref[sub, pl.ds(ks, win), W_B + c:W_B + c + LANES]
                s_ref[len(jobs)] = _pair_scores(q128, k128, rope_order=True)
                jobs.append((q0, ks, c))
        for slot, (q0, ks, c) in enumerate(jobs):
            v128 = qkv_ref[sub, pl.ds(ks, win), 2 * W_B + c:2 * W_B + c + LANES]
            o, lse = _pair_softmax_pv(s_ref[slot], bias_ref[(q0 - ks) // 64], v128, True)
            o_ref[sub, pl.ds(q0, qb), c:c + LANES] = o.astype(o_ref.dtype)
            lse_ref[sub, pl.ds(q0, qb), c:c + LANES] = lse
        return carry

    lax.fori_loop(0, qkv_ref.shape[0] * per_seq, body, 0)


def _dil_bias_table(qb, win, radius):
    nvar = (win - qb) // 64 + 1
    i = (np.arange(2 * qb) % qb)[None, :, None]
    j = np.arange(win)[None, None, :]
    delta = (64 * np.arange(nvar))[:, None, None]
    return jnp.asarray(np.where(np.abs(i + delta - j) <= radius, 0.0, NEG_INF), F32)


def _dilated_group(qkv, window):
    b, r, length, _ = qkv.shape
    radius = window // (2 * r)
    qb = min(128, length)
    win = min(qb + 2 * radius, length)
    assert length % qb == 0 and radius % 64 == 0 and qb % 64 == 0
    table = _dil_bias_table(qb, win, radius)
    group = _largest_divisor(length // qb, (4, 2))
    nsub = max(1, r // 4)
    sub = lambda width: pl.BlockSpec((None, nsub, length, width), lambda i, m: (i, m, 0, 0))
    return pl.pallas_call(
        functools.partial(_dil_kernel, length=length, qb=qb, win=win, radius=radius, group=group),
        grid=(b, r // nsub),
        in_specs=[sub(W_DIL_GROUP), _const_spec(table.shape)],
        out_specs=[sub(W_B), sub(W_B)],
        out_shape=[jax.ShapeDtypeStruct((b, r, length, W_B), BF16),
                   jax.ShapeDtypeStruct((b, r, length, W_B), F32)],
        scratch_shapes=[pltpu.VMEM((group * DIL_HEADS // 2, 2 * qb, win), F32)],
        compiler_params=_params("parallel", "parallel"),
        name=f"dilated_attention_r{r}",
    )(qkv, table)


def _nat_bias_table(rpb):
    qc = np.arange(GRID_W)[:, None]
    kc = np.arange(GRID_W)[None, :]
    win_start = np.clip(qc - NAT_KW // 2, 0, GRID_W - NAT_KW)
    allowed = (kc >= win_start) & (kc < win_start + NAT_KW)
    n_row = 2 * NAT_KH_MAX - 1
    side = GRID_W - NAT_KW
    g = jnp.pad(rpb, ((0, 0), (0, 0), (side, side + 1)))
    a = jnp.broadcast_to(g[:, :, None, :], (NAT_HEADS, n_row, GRID_W, 2 * GRID_W))
    a = a.reshape(NAT_HEADS, n_row, 2 * GRID_W * GRID_W)
    a = a[:, :, GRID_W - 1:GRID_W - 1 + GRID_W * (2 * GRID_W - 1)]
    toep = a.reshape(NAT_HEADS, n_row, GRID_W, 2 * GRID_W - 1)[..., :GRID_W]
    toep = jnp.where(allowed[None, None], toep, NEG_INF).astype(F32)
    return jnp.concatenate([toep[:, :-1], toep[:, 1:]], axis=-1)


def _nat_kernel(q_ref, k_ref, v_ref, bias_ref, o_ref, s_ref, *, rows, group):
    kh = NAT_KH_MAX

    def body(i, carry):
        jobs = []
        for u in range(group):
            r = i * group + u
            rs = jnp.clip(r - kh // 2, 0, rows - kh)
            q0 = pl.multiple_of(r * GRID_W, GRID_W)
            k0 = pl.multiple_of(rs * GRID_W, GRID_W)
            s_ref[u] = _pair_scores(q_ref[0, pl.ds(q0, GRID_W), :],
                                    k_ref[0, pl.ds(k0, kh * GRID_W), :])
            jobs.append((q0, k0, rs - r + NAT_KH_MAX - 1))
        for u, (q0, k0, var) in enumerate(jobs):
            v128 = v_ref[0, pl.ds(k0, kh * GRID_W), :]
            bias = jnp.concatenate(
                [jnp.concatenate([bias_ref[hh, var + j] for j in range(0, kh, 2)], axis=1)
                 for hh in range(2)], axis=0)
            o, _ = _pair_softmax_pv(s_ref[u], bias, v128, False)
            o_ref[0, pl.ds(q0, GRID_W), :] = o.astype(o_ref.dtype)
        return carry

    lax.fori_loop(0, rows // group, body, 0)


def _nat(qkv, rpb):
    b, s, _ = qkv.shape
    rows = s // GRID_W
    assert rows >= NAT_KH_MAX and NAT_KH_MAX % 2 == 0
    table = _nat_bias_table(rpb)
    npair = NAT_HEADS // 2
    col = lambda off: pl.BlockSpec((1, s, LANES), lambda i, p: (i, 0, off + p))
    group = _largest_divisor(rows, (32, 16, 8, 4, 2))
    return pl.pallas_call(
        functools.partial(_nat_kernel, rows=rows, group=group),
        grid=(b, npair),
        in_specs=[col(0), col(npair), col(2 * npair),
                  pl.BlockSpec((2,) + table.shape[1:], lambda i, p: (p, 0, 0, 0))],
        out_specs=pl.BlockSpec((1, s, LANES), lambda i, p: (i, 0, p)),
        out_shape=jax.ShapeDtypeStruct((b, s, W_C), BF16),
        scratch_shapes=[pltpu.VMEM((group, 2 * GRID_W, NAT_KH_MAX * GRID_W), F32)],
        compiler_params=_params("parallel", "parallel"),
        name="neighbourhood_attention",
    )(qkv, qkv, qkv, table)


def _split3(x):
    hi = x.astype(BF16)
    r1 = x - hi.astype(F32)
    mid = r1.astype(BF16)
    lo = (r1 - mid.astype(F32)).astype(BF16)
    return hi, mid, lo


def _gla_kernel(q_ref, k_ref, v_ref, laf_ref, lab_ref, g_ref, o_ref, accf_ref, accb_ref, st_ref,
                *, seq):
    c = GLA_CHUNK
    nchunk = seq // c
    ri = lax.broadcasted_iota(jnp.int32, (c, c), 0)
    ci = lax.broadcasted_iota(jnp.int32, (c, c), 1)
    r2 = lax.broadcasted_iota(jnp.int32, (2 * c, 2 * c), 0)
    c2 = lax.broadcasted_iota(jnp.int32, (2 * c, 2 * c), 1)
    same_head = (r2 >= c) == (c2 >= c)
    zero_v = jnp.zeros((c, GLA_DV), BF16)
    st_ref[...] = jnp.zeros_like(st_ref)

    dirs = ((laf_ref, accf_ref, ci <= ri, same_head & (c2 <= r2), c - 1, c // 2 - 1),
            (lab_ref, accb_ref, ci >= ri, same_head & (c2 >= r2), 0, c // 2))
    group = _largest_divisor(nchunk, (16, 8, 4, 2))

    def body(i, carry):
        jobs = []
        for d, (la_ref, acc_ref, tri, keep, total_row, mid_row) in enumerate(dirs):
            for u in range(group):
                n = i * group + u
                r0 = pl.multiple_of((nchunk - 1 - n if d else n) * c, c)
                hi, mid, lo = _split3(la_ref[0, pl.ds(r0, c), :])
                tri_b = tri.astype(BF16)
                bcum = _dot(tri_b, hi) + _dot(tri_b, mid) + _dot(tri_b, lo)
                jobs.append(dict(d=d, r0=r0, bcum=bcum, keep=keep, acc=acc_ref,
                                 b_tot=bcum[total_row:total_row + 1, :],
                                 b_mid=bcum[mid_row:mid_row + 1, :]))
        for j in jobs:
            q = q_ref[0, pl.ds(j["r0"], c), :]
            k = k_ref[0, pl.ds(j["r0"], c), :]
            v = v_ref[0, pl.ds(j["r0"], c), :]
            q_in = _stack_heads(q * jnp.exp(j["bcum"] - j["b_mid"])).astype(BF16)
            k_in = _stack_heads(k * jnp.exp(j["b_mid"] - j["bcum"])).astype(BF16)
            k_dec = (k * jnp.exp(j["b_tot"] - j["bcum"])).astype(BF16)
            j["q_st"] = _stack_heads(q * jnp.exp(j["bcum"])).astype(BF16)
            j["kv_t"] = _dot(v.astype(F32).T.astype(BF16), k_dec)
            j["att"] = _dot_nt(q_in, k_in)
            j["v2"] = jnp.concatenate([jnp.concatenate([v[:, :GLA_DV], zero_v], axis=1),
                                       jnp.concatenate([zero_v, v[:, GLA_DV:]], axis=1)], axis=0)
        states = [st_ref[0], st_ref[1]]
        for j in jobs:
            state = states[j["d"]]
            j["inter"] = _dot_nt(j["q_st"], state.astype(BF16))
            states[j["d"]] = state * jnp.exp(j["b_tot"]) + j["kv_t"]
        st_ref[0], st_ref[1] = states
        for j in jobs:
            att = jnp.where(j["keep"], j["att"], 0.0).astype(BF16)
            tot = _dot(att, j["v2"]) + j["inter"]
            j["acc"][pl.ds(j["r0"], c), :] = jnp.concatenate(
                [tot[:c, :GLA_DV], tot[c:, GLA_DV:]], axis=1)
        return carry

    lax.fori_loop(0, nchunk // group, body, 0)
    for hh in range(2):
        sl = slice(hh * GLA_DV, (hh + 1) * GLA_DV)
        o_ref[0, :, sl] = _rms(accf_ref[:, sl] + accb_ref[:, sl], g_ref[:, sl]).astype(o_ref.dtype)


def _gla(gq, gk, gv, la, gain):
    b, s, _ = gq.shape
    npair = GLA_HEADS // 2
    qk = pl.BlockSpec((1, s, LANES), lambda i, p: (i, 0, p))
    return pl.pallas_call(
        functools.partial(_gla_kernel, seq=s),
        grid=(b, npair),
        in_specs=[qk, qk,
                  pl.BlockSpec((1, s, 2 * GLA_DV), lambda i, p: (i, 0, p)),
                  pl.BlockSpec((1, s, LANES), lambda i, p: (i, 0, p)),
                  pl.BlockSpec((1, s, LANES), lambda i, p: (i, 0, npair + p)),
                  pl.BlockSpec((1, 2 * GLA_DV), lambda i, p: (0, p))],
        out_specs=pl.BlockSpec((1, s, 2 * GLA_DV), lambda i, p: (i, 0, p)),
        out_shape=jax.ShapeDtypeStruct((b, s, W_D), BF16),
        scratch_shapes=[pltpu.VMEM((s, 2 * GLA_DV), F32), pltpu.VMEM((s, 2 * GLA_DV), F32),
                        pltpu.VMEM((2, 2 * GLA_DV, LANES), F32)],
        compiler_params=_params("parallel", "parallel"),
        name="gated_linear_attention",
    )(gq, gk, gv, la, la, gain)


def _merge_kernel(x_ref, g_ref, oa_ref, ob0_ref, ob1_ref, ob2_ref, l0_ref, l1_ref, l2_ref,
                  oc_ref, od_ref, wz_ref, wm_ref, bm_ref, wp_ref, wo_ref, fg_ref, perm_ref, y_ref,
                  *, final):
    blocks = [_merge_block(blk, x_ref, g_ref, oa_ref, (ob0_ref, ob1_ref, ob2_ref),
                           (l0_ref, l1_ref, l2_ref), oc_ref, od_ref, wz_ref, wm_ref, bm_ref,
                           wp_ref, wo_ref, fg_ref, perm_ref, y_ref, final)
              for blk in range(x_ref.shape[0] // PERM_BLOCK)]
    _interleave(blocks, lag=MERGE_LAG)


def _merge_block(blk, x_ref, g_ref, oa_ref, ob_refs, l_refs, oc_ref, od_ref, wz_ref, wm_ref,
                 bm_ref, wp_ref, wo_ref, fg_ref, perm_ref, y_ref, final):
    rows = slice(blk * PERM_BLOCK, (blk + 1) * PERM_BLOCK)
    x = x_ref[rows, :]
    d = x.shape[1]
    hb = _rms(x, g_ref[...]).astype(BF16)
    yield

    obs, lses = [], []
    for (_, r), o_ref, l_ref in zip(DIL_PATTERNS, ob_refs, l_refs):
        if r == 1:
            obs.append(o_ref[0, 0, rows, :].astype(F32))
            lses.append(l_ref[0, 0, rows, :])
            continue
        pinv = perm_ref[_perm_slot(r)]
        per = PERM_BLOCK // r
        sub_rows = slice(blk * per, (blk + 1) * per)
        obs.append(_dot(pinv, o_ref[0, :, sub_rows, :].reshape(PERM_BLOCK, W_B)))
        parts = _split3(l_ref[0, :, sub_rows, :].reshape(PERM_BLOCK, W_B))
        lses.append(sum(_dot(pinv, part) for part in parts))
        yield

    mx = jnp.maximum(jnp.maximum(lses[0], lses[1]), lses[2])
    es = [jnp.exp(l - mx) for l in lses]
    tot = es[0] + es[1] + es[2]
    o_b = sum(e / tot * o for e, o in zip(es, obs))
    yield

    branches = ((oa_ref, W_A), (None, W_B), (oc_ref, W_C), (od_ref, W_D))
    mixed = [jnp.zeros((PERM_BLOCK, MERGE_GATE_COLS), F32) for _ in range(d // MERGE_GATE_COLS)]
    off = 0
    for i, (o_ref, width) in enumerate(branches):
        o = o_b if o_ref is None else o_ref[rows, :].astype(F32)
        z = _dot(hb, wz_ref[:, off:off + width])
        u = (o * (z * _sigmoid(z))).astype(BF16)
        yield
        proj = _dot(u, wp_ref[off:off + width, :])
        yield
        for n, c in enumerate(range(0, d, MERGE_GATE_COLS)):
            cols = slice(i * d + c, i * d + c + MERGE_GATE_COLS)
            gate = _sigmoid(_dot(hb, wm_ref[:, cols]) + bm_ref[:, cols])
            mixed[n] = mixed[n] + gate * proj[:, c:c + MERGE_GATE_COLS]
            yield
        off += width
    y = x + _dot(jnp.concatenate(mixed, axis=1).astype(BF16), wo_ref[...])
    if final:
        y = _rms(y, fg_ref[...])
    y_ref[rows, :] = y


def _merge(x2, seq, g, oa, obs, lses, oc, od, wz, wm, bm, wp, wo, fg, final, tm):
    n, d = x2.shape
    nseq = seq // tm
    row = lambda width: pl.BlockSpec((tm, width), lambda i: (i, 0))
    sub = lambda t: pl.BlockSpec((1, t.shape[1], tm // t.shape[1], W_B),
                                 lambda i: (i // nseq, 0, i % nseq, 0))
    consts = (wz, wm, bm, wp, wo, fg, _perm_tables(True))
    return pl.pallas_call(
        functools.partial(_merge_kernel, final=final),
        grid=(n // tm,),
        in_specs=[row(d), _const_spec(g.shape), row(W_A)] + [sub(t) for t in (*obs, *lses)]
                 + [row(W_C), row(W_D)] + [_const_spec(a.shape) for a in consts],
        out_specs=row(d),
        out_shape=jax.ShapeDtypeStruct((n, d), F32),
        compiler_params=_params("parallel"),
        name="gate_merge_out",
    )(x2, g, oa, *obs, *lses, oc, od, *consts)


def _rope_tables(seq):
    inv = jnp.power(ROPE_THETA, -jnp.arange(0, MLA_ROPE, 2, dtype=F32) / MLA_ROPE)
    ang = jnp.arange(seq, dtype=jnp.int32).astype(F32)[:, None] * inv[None, :]
    cos, sin = jnp.cos(ang), jnp.sin(ang)
    return jnp.concatenate([cos] * 4, axis=-1), jnp.concatenate([-sin, -sin, sin, sin], axis=-1)


def _prep_layer(w_in, w_uq, w_ukv, w_gf, b_gf, w_gb, b_gb):
    d = w_in.shape[0]
    o = _IN_OFF
    zpad = lambda width: jnp.zeros((d, width), w_in.dtype)
    dil = w_in[:, o[3]:o[4]].reshape(d, len(DIL_PATTERNS), 3, W_B)
    dil = jnp.concatenate([_rope_cols(dil[:, :, :2]), dil[:, :, 2:]], axis=2)
    w_pad = (
        jnp.concatenate([
            w_in[:, o[0]:o[2]],
            _rope_cols(jnp.concatenate([w_in[:, o[2]:o[3]], zpad(LANES - MLA_ROPE)], axis=1))],
            axis=1).astype(BF16),
        dil.reshape(d, 3 * W_DIL_GROUP).astype(BF16),
        w_in[:, o[4]:o[10]].astype(BF16),
        jnp.concatenate([w_in[:, o[10]:o[12]], zpad(LANES - 2 * GLA_GATE_RANK)],
                        axis=1).astype(BF16))
    w_z = w_in[:, o[12]:o[16]].astype(BF16)
    uq = w_uq.reshape(MLA_Q_RANK, MLA_HEADS, MLA_NOPE + MLA_ROPE)
    uq = jnp.pad(uq, ((0, 0), (0, 0), (0, MLA_HEAD_STRIDE - MLA_NOPE - MLA_ROPE)))
    uq = jnp.concatenate([uq[:, :, :MLA_NOPE], _rope_cols(uq[:, :, MLA_NOPE:])], axis=2)
    uq = uq.reshape(MLA_Q_RANK, MLA_HEADS * MLA_HEAD_STRIDE).astype(BF16)
    ukv = w_ukv.reshape(MLA_KV_RANK, MLA_HEADS, MLA_NOPE + MLA_V)
    ukv = jnp.concatenate([ukv[:, :, :MLA_NOPE].reshape(MLA_KV_RANK, -1),
                           ukv[:, :, MLA_NOPE:].reshape(MLA_KV_RANK, -1)], axis=1).astype(BF16)
    wgate = jnp.zeros((LANES, 2 * W_GLA_QK), F32)
    wgate = wgate.at[:GLA_GATE_RANK, :W_GLA_QK].set(w_gf)
    wgate = wgate.at[GLA_GATE_RANK:2 * GLA_GATE_RANK, W_GLA_QK:].set(w_gb)
    bgate = jnp.concatenate([b_gf, b_gb])[None, :]
    return w_pad, w_z, uq, ukv, wgate.astype(BF16), bgate


def kernel(x, norm_g, w_in, mla_q_norm_g, mla_w_uq, mla_kv_norm_g, mla_w_ukv, nat_rpb,
           gla_w_gate_f, gla_b_gate_f, gla_w_gate_b, gla_b_gate_b, gla_norm_g,
           w_proj_a, w_proj_b, w_proj_c, w_proj_d, w_merge, b_merge, w_out, final_norm_g):
    b, s, d = x.shape
    depth = w_in.shape[0]
    tm = min(512, s)
    tq = min(1024, s)
    tabs = _rope_tables(s)
    x2 = x.reshape(b * s, d)
    w_in_b = w_in.astype(BF16)
    for l in range(depth):
        w_pad, w_z, uq, ukv, wgate, bgate = _prep_layer(
            w_in_b[l], mla_w_uq[l], mla_w_ukv[l], gla_w_gate_f[l], gla_b_gate_f[l],
            gla_w_gate_b[l], gla_b_gate_b[l])
        (q, kc, va, d0, d1, d2, nat_qkv, gq, gk, gv, la) = _in_proj(
            x2, s, tabs, norm_g[l][None, :], w_pad, mla_q_norm_g[l][None, :], uq,
            mla_kv_norm_g[l][None, :], ukv, wgate, bgate, tm)
        sh = lambda t: t.reshape(b, s, t.shape[-1])
        o_a = _mla(sh(q), sh(kc), sh(va), tq)
        obs, lses = [], []
        for dil_qkv, (window, _) in zip((d0, d1, d2), DIL_PATTERNS):
            o_g, lse_g = _dilated_group(dil_qkv, window)
            obs.append(o_g)
            lses.append(lse_g)
        o_c = _nat(sh(nat_qkv), nat_rpb[l])
        o_d = _gla(sh(gq), sh(gk), sh(gv), sh(la), gla_norm_g[l].reshape(1, W_D))
        w_p = jnp.concatenate([w_proj_a[l], w_proj_b[l], w_proj_c[l], w_proj_d[l]], axis=0)
        x2 = _merge(x2, s, norm_g[l][None, :], o_a.reshape(b * s, W_A), obs, lses,
                    o_c.reshape(b * s, W_C), o_d.reshape(b * s, W_D), w_z,
                    w_merge[l].astype(BF16), b_merge[l][None, :], w_p.astype(BF16),
                    w_out[l].astype(BF16), final_norm_g[None, :], l == depth - 1, tm)
    return x2.reshape(b, s, d)
```

```python
import functools

import numpy as np
import jax
import jax.numpy as jnp
from jax import lax
from jax.experimental import pallas as pl
from jax.experimental.pallas import tpu as pltpu

F32 = jnp.float32
BF16 = jnp.bfloat16

GRID_W = 64
ROPE_THETA = 10000.0
NORM_EPS = 1e-6
NEG_INF = -1e30

MLA_HEADS = 4
MLA_NOPE = 128
MLA_ROPE = 64
MLA_V = 128
MLA_Q_RANK = 256
MLA_KV_RANK = 128
DIL_PATTERNS = ((128, 1), (512, 4), (2048, 16))
DIL_HEADS = 4
DIL_DH = 64
NAT_HEADS = 8
NAT_DH = 64
NAT_KH_MAX = 8
NAT_KW = 16
GLA_HEADS = 4
GLA_DK = 64
GLA_DV = 128
GLA_GATE_RANK = 16
GLA_TAU = 16.0
GLA_CHUNK = 64

W_A = MLA_HEADS * MLA_V
W_B = DIL_HEADS * DIL_DH
W_C = NAT_HEADS * NAT_DH
W_D = GLA_HEADS * GLA_DV
N_BRANCH = 4
W_DIL_GROUP = 3 * W_B
W_GLA_QK = GLA_HEADS * GLA_DK

LOG2_E = 1.4426950408889634
LANES = 128
MLA_HEAD_STRIDE = 256
MLA_SUB_QUERIES = 512


def _largest_divisor(n, candidates):
    return next((c for c in candidates if n % c == 0), 1)
VMEM_LIMIT = 56 * 1024 * 1024

_IN_SIZES = (MLA_Q_RANK, MLA_KV_RANK, MLA_ROPE, 3 * W_DIL_GROUP, W_C, W_C, W_C,
             W_GLA_QK, W_GLA_QK, W_D, GLA_GATE_RANK, GLA_GATE_RANK, W_A, W_B, W_C, W_D)
_IN_OFF = tuple(int(v) for v in np.concatenate([[0], np.cumsum(_IN_SIZES)]))

_P_Q = 0
_P_KV = _P_Q + MLA_Q_RANK
_P_KR = _P_KV + MLA_KV_RANK
_P_DIL = _P_KR + LANES
_P_NAT = _P_DIL + 3 * W_DIL_GROUP
_P_GQ = _P_NAT + 3 * W_C
_P_GK = _P_GQ + W_GLA_QK
_P_GV = _P_GK + W_GLA_QK
_P_GG = _P_GV + W_D
_P_END = _P_GG + LANES
_PIECE_STARTS = (_P_Q, _P_DIL, _P_NAT, _P_GG)


def _dot(a, b):
    return jnp.dot(a, b, preferred_element_type=F32)


def _dot_nt(a, b):
    return lax.dot_general(a, b, (((1,), (1,)), ((), ())), preferred_element_type=F32)


def _rms(x, g):
    return x * lax.rsqrt(jnp.mean(x * x, axis=-1, keepdims=True) + NORM_EPS) * g


def _rope_cols(w):
    half = MLA_ROPE // 2
    t = w.reshape(w.shape[:-1] + (w.shape[-1] // LANES, 2, 2, half))
    return jnp.swapaxes(t, -3, -2).reshape(w.shape)


def _rope128(x, cos, sin_signed):
    return x * cos + pltpu.roll(x, LANES // 2, 1) * sin_signed


def _sigmoid(x):
    return 1.0 / (1.0 + jnp.exp(-x))


def _log_sigmoid(x):
    return jnp.minimum(x, 0.0) - jnp.log1p(jnp.exp(-jnp.abs(x)))


def _params(*sem):
    return pltpu.CompilerParams(dimension_semantics=sem, vmem_limit_bytes=VMEM_LIMIT)


def _const_spec(shape):
    nd = len(shape)
    return pl.BlockSpec(shape, lambda *_: (0,) * nd, pipeline_mode=pl.Buffered(1))


_PERM_DILATIONS = tuple(r for _, r in DIL_PATTERNS if r > 1)


def _perm_slot(r):
    return _PERM_DILATIONS.index(r)


PERM_BLOCK = 256
IN_PROJ_NORM_ROWS = 64
IN_PROJ_LAG = 5
IN_PROJ_ORDER = "EM012NG"
MERGE_GATE_COLS = 512
MERGE_LAG = 2


def _perm_tables(inverse):
    tabs = []
    for r in _PERM_DILATIONS:
        p = np.zeros((PERM_BLOCK, PERM_BLOCK), np.float32)
        tok = np.arange(PERM_BLOCK)
        p[(tok % r) * (PERM_BLOCK // r) + tok // r, tok] = 1.0
        tabs.append(p.T if inverse else p)
    return jnp.asarray(np.stack(tabs), BF16)


def _interleave(gens, lag):
    gens = list(gens)
    alive = [True] * len(gens)
    step = 0
    while any(alive):
        for i, gen in enumerate(gens):
            if alive[i] and step >= i * lag:
                try:
                    next(gen)
                except StopIteration:
                    alive[i] = False
        step += 1


def _in_proj_kernel(x_ref, cos_ref, sin_ref, g_ref, w0_ref, w1_ref, w2_ref, w3_ref, qg_ref, wuq_ref, kvg_ref,
                    wukv_ref, wgate_ref, bgate_ref, perm_ref,
                    q_ref, kc_ref, va_ref, d0_ref, d1_ref, d2_ref, nat_ref, gq_ref, gk_ref,
                    gv_ref, la_ref, y_ref):
    w_refs = (w0_ref, w1_ref, w2_ref, w3_ref)
    blocks = [_in_proj_block(blk, x_ref, cos_ref, sin_ref, g_ref, w_refs, qg_ref, wuq_ref,
                             kvg_ref, wukv_ref, wgate_ref, bgate_ref, perm_ref, q_ref, kc_ref,
                             va_ref, (d0_ref, d1_ref, d2_ref), nat_ref, gq_ref, gk_ref, gv_ref,
                             la_ref, y_ref)
              for blk in range(x_ref.shape[0] // PERM_BLOCK)]
    _interleave(blocks, lag=IN_PROJ_LAG)


def _in_proj_block(blk, x_ref, cos_ref, sin_ref, g_ref, w_refs, qg_ref, wuq_ref, kvg_ref,
                   wukv_ref, wgate_ref, bgate_ref, perm_ref, q_ref, kc_ref, va_ref, d_refs,
                   nat_ref, gq_ref, gk_ref, gv_ref, la_ref, y_ref):
    r0 = blk * PERM_BLOCK
    rows = slice(r0, r0 + PERM_BLOCK)
    chunks = []
    for c0 in range(r0, r0 + PERM_BLOCK, IN_PROJ_NORM_ROWS):
        chunks.append(_rms(x_ref[c0:c0 + IN_PROJ_NORM_ROWS, :], g_ref[...]).astype(BF16))
        yield
    hb = jnp.concatenate(chunks, axis=0)
    cos, sin = cos_ref[rows, :], sin_ref[rows, :]

    def mm(a, width):
        k = max(i for i, start in enumerate(_PIECE_STARTS) if start <= a)
        off = a - _PIECE_STARTS[k]
        return _dot(hb, w_refs[k][:, off:off + width])

    def rope(t):
        return _rope128(t, cos, sin)

    def sec_mla():
        qn = _rms(mm(_P_Q, MLA_Q_RANK), qg_ref[...]).astype(BF16)
        yield
        for h in range(MLA_HEADS):
            base = h * MLA_HEAD_STRIDE
            qh = _dot(qn, wuq_ref[:, base:base + MLA_HEAD_STRIDE])
            q_ref[rows, base:base + LANES] = qh[:, :LANES].astype(BF16)
            q_ref[rows, base + LANES:base + 2 * LANES] = rope(qh[:, LANES:]).astype(BF16)
            yield
        kv_kr = mm(_P_KV, MLA_KV_RANK + LANES)
        kvn = _rms(kv_kr[:, :MLA_KV_RANK], kvg_ref[...]).astype(BF16)
        kpe = rope(kv_kr[:, MLA_KV_RANK:]).astype(BF16)
        yield
        k_nope = _dot(kvn, wukv_ref[:, :W_A])
        for h in range(MLA_HEADS):
            base = h * MLA_HEAD_STRIDE
            kc_ref[rows, base:base + LANES] = k_nope[:, h * LANES:(h + 1) * LANES].astype(BF16)
            kc_ref[rows, base + LANES:base + 2 * LANES] = kpe
        yield
        va_ref[rows, :] = _dot(kvn, wukv_ref[:, W_A:2 * W_A]).astype(BF16)
        yield

    def sec_dil(g):
        d_ref, (_, r) = d_refs[g], DIL_PATTERNS[g]
        base = _P_DIL + g * W_DIL_GROUP
        qk = mm(base, 2 * W_B)
        tiles = [(rope(qk[:, c:c + LANES]) * (DIL_DH ** -0.5 if c < W_B else 1.0)).astype(BF16)
                 for c in range(0, 2 * W_B, LANES)]
        if r == 1:
            for t, c in zip(tiles, range(0, 2 * W_B, LANES)):
                d_ref[0, 0, rows, c:c + LANES] = t
            yield
            d_ref[0, 0, rows, 2 * W_B:] = mm(base + 2 * W_B, W_B).astype(BF16)
            yield
            return
        for t, c in zip(tiles, range(0, 2 * W_B, LANES)):
            y_ref[g, blk, :, c:c + LANES] = t
        yield
        y_ref[g, blk, :, 2 * W_B:] = mm(base + 2 * W_B, W_B).astype(BF16)
        per = PERM_BLOCK // r
        y = _dot(perm_ref[_perm_slot(r)], y_ref[g, blk]).astype(BF16)
        d_ref[0, :, blk * per:(blk + 1) * per, :] = y.reshape(r, per, W_DIL_GROUP)
        yield

    def sec_nat():
        nat_ref[rows, :W_C] = (mm(_P_NAT, W_C) * (NAT_DH ** -0.5)).astype(BF16)
        yield
        for c in range(W_C, 3 * W_C, W_C):
            nat_ref[rows, c:c + W_C] = mm(_P_NAT + c, W_C).astype(BF16)
            yield

    def sec_gla():
        gq_ref[rows, :] = mm(_P_GQ, W_GLA_QK) * (GLA_DK ** -0.5)
        yield
        gk_ref[rows, :] = mm(_P_GK, W_GLA_QK)
        yield
        gv_ref[rows, :] = mm(_P_GV, W_D).astype(BF16)
        yield

    def sec_gates():
        gates = _dot(mm(_P_GG, LANES).astype(BF16), wgate_ref[...]) + bgate_ref[...]
        for c in range(0, 2 * W_GLA_QK, LANES):
            yield
            la_ref[rows, c:c + LANES] = _log_sigmoid(gates[:, c:c + LANES]) * (1.0 / GLA_TAU)
        yield

    sections = {"M": sec_mla, "0": lambda: sec_dil(0), "1": lambda: sec_dil(1),
                "2": lambda: sec_dil(2), "N": sec_nat, "G": sec_gla, "E": sec_gates}
    for key in IN_PROJ_ORDER:
        yield from sections[key]()


def _in_proj(x2, seq, tabs, g, w, qg, wuq, kvg, wukv, wgate, bgate, tm):
    n, d = x2.shape
    nseq = seq // tm
    batch = n // seq
    row = lambda width: pl.BlockSpec((tm, width), lambda i: (i, 0))
    tab = pl.BlockSpec((tm, LANES), lambda i: (i % nseq, 0))
    sub = lambda r, width: pl.BlockSpec((1, r, tm // r, width),
                                        lambda i: (i // nseq, 0, i % nseq, 0))
    sub_shape = lambda r, width, dt: jax.ShapeDtypeStruct((batch, r, seq // r, width), dt)
    dil = [r for _, r in DIL_PATTERNS]
    outs = [(4 * MLA_HEAD_STRIDE, BF16), (4 * MLA_HEAD_STRIDE, BF16), (W_A, BF16)]
    outs2 = [(3 * W_C, BF16), (W_GLA_QK, F32), (W_GLA_QK, F32), (W_D, BF16), (2 * W_GLA_QK, F32)]
    assert tm % PERM_BLOCK == 0
    perm = _perm_tables(False)
    consts = (g, *w, qg, wuq, kvg, wukv, wgate, bgate, perm)
    return pl.pallas_call(
        _in_proj_kernel,
        grid=(n // tm,),
        in_specs=[row(d), tab, tab] + [_const_spec(a.shape) for a in consts],
        out_specs=([row(wd) for wd, _ in outs] + [sub(r, W_DIL_GROUP) for r in dil]
                   + [row(wd) for wd, _ in outs2]),
        out_shape=([jax.ShapeDtypeStruct((n, wd), dt) for wd, dt in outs]
                   + [sub_shape(r, W_DIL_GROUP, BF16) for r in dil]
                   + [jax.ShapeDtypeStruct((n, wd), dt) for wd, dt in outs2]),
        scratch_shapes=[pltpu.VMEM((len(dil), tm // PERM_BLOCK, PERM_BLOCK, W_DIL_GROUP), BF16)],
        compiler_params=_params("parallel"),
        name="in_proj",
    )(x2, *tabs, *consts)


def _mla_kernel(q_ref, k_ref, v_ref, o_ref, vx_ref):
    @pl.when(pl.program_id(2) == 0)
    def _():
        vx_ref[:, :MLA_V] = v_ref[0]
        vx_ref[:, MLA_V:] = jnp.ones((vx_ref.shape[0], MLA_V), BF16)

    c = (MLA_NOPE + MLA_ROPE) ** -0.5 * LOG2_E
    sub = MLA_SUB_QUERIES
    nsub = q_ref.shape[1] // sub
    scores = [_dot_nt(q_ref[0, i * sub:(i + 1) * sub, :], k_ref[0]) for i in range(nsub)]
    for i, s in enumerate(scores):
        m = jnp.max(s, axis=-1, keepdims=True)
        p = jnp.exp2((s - m) * c).astype(BF16)
        ov = _dot(p, vx_ref[...])
        o_ref[0, i * sub:(i + 1) * sub, :] = (ov[:, :MLA_V] / ov[:, MLA_V:]).astype(o_ref.dtype)


def _mla(q, kc, va, tq):
    b, s, _ = q.shape
    return pl.pallas_call(
        _mla_kernel,
        grid=(b, MLA_HEADS, s // tq),
        in_specs=[pl.BlockSpec((1, tq, MLA_HEAD_STRIDE), lambda i, h, j: (i, j, h)),
                  pl.BlockSpec((1, s, MLA_HEAD_STRIDE), lambda i, h, j: (i, 0, h)),
                  pl.BlockSpec((1, s, MLA_V), lambda i, h, j: (i, 0, h))],
        out_specs=pl.BlockSpec((1, tq, MLA_V), lambda i, h, j: (i, j, h)),
        out_shape=jax.ShapeDtypeStruct((b, s, W_A), BF16),
        scratch_shapes=[pltpu.VMEM((s, 2 * MLA_V), BF16)],
        compiler_params=_params("parallel", "parallel", "arbitrary"),
        name="mla_attention",
    )(q, kc, va)


def _stack_heads(x128, rope_order=False):
    lane = lax.broadcasted_iota(jnp.int32, x128.shape, 1)
    first = ((lane & 32) == 0) if rope_order else (lane < 64)
    zero = jnp.zeros_like(x128)
    return jnp.concatenate([jnp.where(first, x128, zero), jnp.where(first, zero, x128)], axis=0)


def _unstack_heads(y2):
    m = y2.shape[0] // 2
    lane = lax.broadcasted_iota(jnp.int32, (m, y2.shape[1]), 1)
    return jnp.where(lane < 64, y2[:m], y2[m:])


def _pair_scores(q128, k128, rope_order=False):
    return _dot_nt(_stack_heads(q128, rope_order), k128)


def _pair_softmax_pv(s, bias, v128, want_lse):
    s = s + bias
    mx = jnp.max(s, axis=-1, keepdims=True)
    p = jnp.exp(s - mx)
    den = jnp.sum(p, axis=-1, keepdims=True)
    o = _unstack_heads(_dot(p.astype(BF16), v128) / den)
    if not want_lse:
        return o, None
    lse = jnp.broadcast_to(mx + jnp.log(den), (s.shape[0], LANES))
    return o, _unstack_heads(lse)


def _dil_kernel(qkv_ref, bias_ref, o_ref, lse_ref, s_ref, *, length, qb, win, radius, group):
    per_seq = length // (qb * group)

    def body(i, carry):
        sub = i // per_seq
        jobs = []
        for u in range(group):
            q0 = pl.multiple_of(((i % per_seq) * group + u) * qb, qb)
            ks = pl.multiple_of(jnp.clip(q0 - radius, 0, length - win), 64)
            for c in range(0, W_B, LANES):
                q128 = qkv_ref[sub, pl.ds(q0, qb), c:c + LANES]
                k128 = qkv_ref[sub, pl.ds(ks, win), W_B + c:W_B + c + LANES]
                s_ref[len(jobs)] = _pair_scores(q128, k128, rope_order=True)
                jobs.append((q0, ks, c))
        for slot, (q0, ks, c) in enumerate(jobs):
            v128 = qkv_ref[sub, pl.ds(ks, win), 2 * W_B + c:2 * W_B + c + LANES]
            o, lse = _pair_softmax_pv(s_ref[slot], bias_ref[(q0 - ks) // 64], v128, True)
            o_ref[sub, pl.ds(q0, qb), c:c + LANES] = o.astype(o_ref.dtype)
            lse_ref[sub, pl.ds(q0, qb), c:c + LANES] = lse
        return carry

    lax.fori_loop(0, qkv_ref.shape[0] * per_seq, body, 0)


def _dil_bias_table(qb, win, radius):
    nvar = (win - qb) // 64 + 1
    i = (np.arange(2 * qb) % qb)[None, :, None]
    j = np.arange(win)[None, None, :]
    delta = (64 * np.arange(nvar))[:, None, None]
    return jnp.asarray(np.where(np.abs(i + delta - j) <= radius, 0.0, NEG_INF), F32)


def _dilated_group(qkv, window):
    b, r, length, _ = qkv.shape
    radius = window // (2 * r)
    qb = min(128, length)
    win = min(qb + 2 * radius, length)
    assert length % qb == 0 and radius % 64 == 0 and qb % 64 == 0
    table = _dil_bias_table(qb, win, radius)
    group = _largest_divisor(length // qb, (4, 2))
    nsub = max(1, r // 4)
    sub = lambda width: pl.BlockSpec((None, nsub, length, width), lambda i, m: (i, m, 0, 0))
    return pl.pallas_call(
        functools.partial(_dil_kernel, length=length, qb=qb, win=win, radius=radius, group=group),
        grid=(b, r // nsub),
        in_specs=[sub(W_DIL_GROUP), _const_spec(table.shape)],
        out_specs=[sub(W_B), sub(W_B)],
        out_shape=[jax.ShapeDtypeStruct((b, r, length, W_B), BF16),
                   jax.ShapeDtypeStruct((b, r, length, W_B), F32)],
        scratch_shapes=[pltpu.VMEM((group * DIL_HEADS // 2, 2 * qb, win), F32)],
        compiler_params=_params("parallel", "parallel"),
        name=f"dilated_attention_r{r}",
    )(qkv, table)


def _nat_bias_table(rpb):
    qc = np.arange(GRID_W)[:, None]
    kc = np.arange(GRID_W)[None, :]
    win_start = np.clip(qc - NAT_KW // 2, 0, GRID_W - NAT_KW)
    allowed = (kc >= win_start) & (kc < win_start + NAT_KW)
    n_row = 2 * NAT_KH_MAX - 1
    side = GRID_W - NAT_KW
    g = jnp.pad(rpb, ((0, 0), (0, 0), (side, side + 1)))
    a = jnp.broadcast_to(g[:, :, None, :], (NAT_HEADS, n_row, GRID_W, 2 * GRID_W))
    a = a.reshape(NAT_HEADS, n_row, 2 * GRID_W * GRID_W)
    a = a[:, :, GRID_W - 1:GRID_W - 1 + GRID_W * (2 * GRID_W - 1)]
    toep = a.reshape(NAT_HEADS, n_row, GRID_W, 2 * GRID_W - 1)[..., :GRID_W]
    toep = jnp.where(allowed[None, None], toep, NEG_INF).astype(F32)
    return jnp.concatenate([toep[:, :-1], toep[:, 1:]], axis=-1)


def _nat_kernel(q_ref, k_ref, v_ref, bias_ref, o_ref, s_ref, *, rows, group):
    kh = NAT_KH_MAX

    def body(i, carry):
        jobs = []
        for u in range(group):
            r = i * group + u
            rs = jnp.clip(r - kh // 2, 0, rows - kh)
            q0 = pl.multiple_of(r * GRID_W, GRID_W)
            k0 = pl.multiple_of(rs * GRID_W, GRID_W)
            s_ref[u] = _pair_scores(q_ref[0, pl.ds(q0, GRID_W), :],
                                    k_ref[0, pl.ds(k0, kh * GRID_W), :])
            jobs.append((q0, k0, rs - r + NAT_KH_MAX - 1))
        for u, (q0, k0, var) in enumerate(jobs):
            v128 = v_ref[0, pl.ds(k0, kh * GRID_W), :]
            bias = jnp.concatenate(
                [jnp.concatenate([bias_ref[hh, var + j] for j in range(0, kh, 2)], axis=1)
                 for hh in range(2)], axis=0)
            o, _ = _pair_softmax_pv(s_ref[u], bias, v128, False)
            o_ref[0, pl.ds(q0, GRID_W), :] = o.astype(o_ref.dtype)
        return carry

    lax.fori_loop(0, rows // group, body, 0)


def _nat(qkv, rpb):
    b, s, _ = qkv.shape
    rows = s // GRID_W
    assert rows >= NAT_KH_MAX and NAT_KH_MAX % 2 == 0
    table = _nat_bias_table(rpb)
    npair = NAT_HEADS // 2
    col = lambda off: pl.BlockSpec((1, s, LANES), lambda i, p: (i, 0, off + p))
    group = _largest_divisor(rows, (32, 16, 8, 4, 2))
    return pl.pallas_call(
        functools.partial(_nat_kernel, rows=rows, group=group),
        grid=(b, npair),
        in_specs=[col(0), col(npair), col(2 * npair),
                  pl.BlockSpec((2,) + table.shape[1:], lambda i, p: (p, 0, 0, 0))],
        out_specs=pl.BlockSpec((1, s, LANES), lambda i, p: (i, 0, p)),
        out_shape=jax.ShapeDtypeStruct((b, s, W_C), BF16),
        scratch_shapes=[pltpu.VMEM((group, 2 * GRID_W, NAT_KH_MAX * GRID_W), F32)],
        compiler_params=_params("parallel", "parallel"),
        name="neighbourhood_attention",
    )(qkv, qkv, qkv, table)


def _split3(x):
    hi = x.astype(BF16)
    r1 = x - hi.astype(F32)
    mid = r1.astype(BF16)
    lo = (r1 - mid.astype(F32)).astype(BF16)
    return hi, mid, lo


def _gla_kernel(q_ref, k_ref, v_ref, laf_ref, lab_ref, g_ref, o_ref, accf_ref, accb_ref, st_ref,
                *, seq):
    c = GLA_CHUNK
    nchunk = seq // c
    ri = lax.broadcasted_iota(jnp.int32, (c, c), 0)
    ci = lax.broadcasted_iota(jnp.int32, (c, c), 1)
    r2 = lax.broadcasted_iota(jnp.int32, (2 * c, 2 * c), 0)
    c2 = lax.broadcasted_iota(jnp.int32, (2 * c, 2 * c), 1)
    same_head = (r2 >= c) == (c2 >= c)
    zero_v = jnp.zeros((c, GLA_DV), BF16)
    st_ref[...] = jnp.zeros_like(st_ref)

    dirs = ((laf_ref, accf_ref, ci <= ri, same_head & (c2 <= r2), c - 1, c // 2 - 1),
            (lab_ref, accb_ref, ci >= ri, same_head & (c2 >= r2), 0, c // 2))
    group = _largest_divisor(nchunk, (16, 8, 4, 2))

    def body(i, carry):
        jobs = []
        for d, (la_ref, acc_ref, tri, keep, total_row, mid_row) in enumerate(dirs):
            for u in range(group):
                n = i * group + u
                r0 = pl.multiple_of((nchunk - 1 - n if d else n) * c, c)
                hi, mid, lo = _split3(la_ref[0, pl.ds(r0, c), :])
                tri_b = tri.astype(BF16)
                bcum = _dot(tri_b, hi) + _dot(tri_b, mid) + _dot(tri_b, lo)
                jobs.append(dict(d=d, r0=r0, bcum=bcum, keep=keep, acc=acc_ref,
                                 b_tot=bcum[total_row:total_row + 1, :],
                                 b_mid=bcum[mid_row:mid_row + 1, :]))
        for j in jobs:
            q = q_ref[0, pl.ds(j["r0"], c), :]
            k = k_ref[0, pl.ds(j["r0"], c), :]
            v = v_ref[0, pl.ds(j["r0"], c), :]
            q_in = _stack_heads(q * jnp.exp(j["bcum"] - j["b_mid"])).astype(BF16)
            k_in = _stack_heads(k * jnp.exp(j["b_mid"] - j["bcum"])).astype(BF16)
            k_dec = (k * jnp.exp(j["b_tot"] - j["bcum"])).astype(BF16)
            j["q_st"] = _stack_heads(q * jnp.exp(j["bcum"])).astype(BF16)
            j["kv_t"] = _dot(v.astype(F32).T.astype(BF16), k_dec)
            j["att"] = _dot_nt(q_in, k_in)
            j["v2"] = jnp.concatenate([jnp.concatenate([v[:, :GLA_DV], zero_v], axis=1),
                                       jnp.concatenate([zero_v, v[:, GLA_DV:]], axis=1)], axis=0)
        states = [st_ref[0], st_ref[1]]
        for j in jobs:
            state = states[j["d"]]
            j["inter"] = _dot_nt(j["q_st"], state.astype(BF16))
            states[j["d"]] = state * jnp.exp(j["b_tot"]) + j["kv_t"]
        st_ref[0], st_ref[1] = states
        for j in jobs:
            att = jnp.where(j["keep"], j["att"], 0.0).astype(BF16)
            tot = _dot(att, j["v2"]) + j["inter"]
            j["acc"][pl.ds(j["r0"], c), :] = jnp.concatenate(
                [tot[:c, :GLA_DV], tot[c:, GLA_DV:]], axis=1)
        return carry

    lax.fori_loop(0, nchunk // group, body, 0)
    for hh in range(2):
        sl = slice(hh * GLA_DV, (hh + 1) * GLA_DV)
        o_ref[0, :, sl] = _rms(accf_ref[:, sl] + accb_ref[:, sl], g_ref[:, sl]).astype(o_ref.dtype)


def _gla(gq, gk, gv, la, gain):
    b, s, _ = gq.shape
    npair = GLA_HEADS // 2
    qk = pl.BlockSpec((1, s, LANES), lambda i, p: (i, 0, p))
    return pl.pallas_call(
        functools.partial(_gla_kernel, seq=s),
        grid=(b, npair),
        in_specs=[qk, qk,
                  pl.BlockSpec((1, s, 2 * GLA_DV), lambda i, p: (i, 0, p)),
                  pl.BlockSpec((1, s, LANES), lambda i, p: (i, 0, p)),
                  pl.BlockSpec((1, s, LANES), lambda i, p: (i, 0, npair + p)),
                  pl.BlockSpec((1, 2 * GLA_DV), lambda i, p: (0, p))],
        out_specs=pl.BlockSpec((1, s, 2 * GLA_DV), lambda i, p: (i, 0, p)),
        out_shape=jax.ShapeDtypeStruct((b, s, W_D), BF16),
        scratch_shapes=[pltpu.VMEM((s, 2 * GLA_DV), F32), pltpu.VMEM((s, 2 * GLA_DV), F32),
                        pltpu.VMEM((2, 2 * GLA_DV, LANES), F32)],
        compiler_params=_params("parallel", "parallel"),
        name="gated_linear_attention",
    )(gq, gk, gv, la, la, gain)


def _merge_kernel(x_ref, g_ref, oa_ref, ob0_ref, ob1_ref, ob2_ref, l0_ref, l1_ref, l2_ref,
                  oc_ref, od_ref, wz_ref, wm_ref, bm_ref, wp_ref, wo_ref, fg_ref, perm_ref, y_ref,
                  *, final):
    blocks = [_merge_block(blk, x_ref, g_ref, oa_ref, (ob0_ref, ob1_ref, ob2_ref),
                           (l0_ref, l1_ref, l2_ref), oc_ref, od_ref, wz_ref, wm_ref, bm_ref,
                           wp_ref, wo_ref, fg_ref, perm_ref, y_ref, final)
              for blk in range(x_ref.shape[0] // PERM_BLOCK)]
    _interleave(blocks, lag=MERGE_LAG)


def _merge_block(blk, x_ref, g_ref, oa_ref, ob_refs, l_refs, oc_ref, od_ref, wz_ref, wm_ref,
                 bm_ref, wp_ref, wo_ref, fg_ref, perm_ref, y_ref, final):
    rows = slice(blk * PERM_BLOCK, (blk + 1) * PERM_BLOCK)
    x = x_ref[rows, :]
    d = x.shape[1]
    hb = _rms(x, g_ref[...]).astype(BF16)
    yield

    obs, lses = [], []
    for (_, r), o_ref, l_ref in zip(DIL_PATTERNS, ob_refs, l_refs):
        if r == 1:
            obs.append(o_ref[0, 0, rows, :].astype(F32))
            lses.append(l_ref[0, 0, rows, :])
            continue
        pinv = perm_ref[_perm_slot(r)]
        per = PERM_BLOCK // r
        sub_rows = slice(blk * per, (blk + 1) * per)
        obs.append(_dot(pinv, o_ref[0, :, sub_rows, :].reshape(PERM_BLOCK, W_B)))
        parts = _split3(l_ref[0, :, sub_rows, :].reshape(PERM_BLOCK, W_B))
        lses.append(sum(_dot(pinv, part) for part in parts))
        yield

    mx = jnp.maximum(jnp.maximum(lses[0], lses[1]), lses[2])
    es = [jnp.exp(l - mx) for l in lses]
    tot = es[0] + es[1] + es[2]
    o_b = sum(e / tot * o for e, o in zip(es, obs))
    yield

    branches = ((oa_ref, W_A), (None, W_B), (oc_ref, W_C), (od_ref, W_D))
    mixed = [jnp.zeros((PERM_BLOCK, MERGE_GATE_COLS), F32) for _ in range(d // MERGE_GATE_COLS)]
    off = 0
    for i, (o_ref, width) in enumerate(branches):
        o = o_b if o_ref is None else o_ref[rows, :].astype(F32)
        z = _dot(hb, wz_ref[:, off:off + width])
        u = (o * (z * _sigmoid(z))).astype(BF16)
        yield
        proj = _dot(u, wp_ref[off:off + width, :])
        yield
        for n, c in enumerate(range(0, d, MERGE_GATE_COLS)):
            cols = slice(i * d + c, i * d + c + MERGE_GATE_COLS)
            gate = _sigmoid(_dot(hb, wm_ref[:, cols]) + bm_ref[:, cols])
            mixed[n] = mixed[n] + gate * proj[:, c:c + MERGE_GATE_COLS]
            yield
        off += width
    y = x + _dot(jnp.concatenate(mixed, axis=1).astype(BF16), wo_ref[...])
    if final:
        y = _rms(y, fg_ref[...])
    y_ref[rows, :] = y


def _merge(x2, seq, g, oa, obs, lses, oc, od, wz, wm, bm, wp, wo, fg, final, tm):
    n, d = x2.shape
    nseq = seq // tm
    row = lambda width: pl.BlockSpec((tm, width), lambda i: (i, 0))
    sub = lambda t: pl.BlockSpec((1, t.shape[1], tm // t.shape[1], W_B),
                                 lambda i: (i // nseq, 0, i % nseq, 0))
    consts = (wz, wm, bm, wp, wo, fg, _perm_tables(True))
    return pl.pallas_call(
        functools.partial(_merge_kernel, final=final),
        grid=(n // tm,),
        in_specs=[row(d), _const_spec(g.shape), row(W_A)] + [sub(t) for t in (*obs, *lses)]
                 + [row(W_C), row(W_D)] + [_const_spec(a.shape) for a in consts],
        out_specs=row(d),
        out_shape=jax.ShapeDtypeStruct((n, d), F32),
        compiler_params=_params("parallel"),
        name="gate_merge_out",
    )(x2, g, oa, *obs, *lses, oc, od, *consts)


def _rope_tables(seq):
    inv = jnp.power(ROPE_THETA, -jnp.arange(0, MLA_ROPE, 2, dtype=F32) / MLA_ROPE)
    ang = jnp.arange(seq, dtype=jnp.int32).astype(F32)[:, None] * inv[None, :]
    cos, sin = jnp.cos(ang), jnp.sin(ang)
    return jnp.concatenate([cos] * 4, axis=-1), jnp.concatenate([-sin, -sin, sin, sin], axis=-1)


def _prep_layer(w_in, w_uq, w_ukv, w_gf, b_gf, w_gb, b_gb):
    d = w_in.shape[0]
    o = _IN_OFF
    zpad = lambda width: jnp.zeros((d, width), w_in.dtype)
    dil = w_in[:, o[3]:o[4]].reshape(d, len(DIL_PATTERNS), 3, W_B)
    dil = jnp.concatenate([_rope_cols(dil[:, :, :2]), dil[:, :, 2:]], axis=2)
    w_pad = (
        jnp.concatenate([
            w_in[:, o[0]:o[2]],
            _rope_cols(jnp.concatenate([w_in[:, o[2]:o[3]], zpad(LANES - MLA_ROPE)], axis=1))],
            axis=1).astype(BF16),
        dil.reshape(d, 3 * W_DIL_GROUP).astype(BF16),
        w_in[:, o[4]:o[10]].astype(BF16),
        jnp.concatenate([w_in[:, o[10]:o[12]], zpad(LANES - 2 * GLA_GATE_RANK)],
                        axis=1).astype(BF16))
    w_z = w_in[:, o[12]:o[16]].astype(BF16)
    uq = w_uq.reshape(MLA_Q_RANK, MLA_HEADS, MLA_NOPE + MLA_ROPE)
    uq = jnp.pad(uq, ((0, 0), (0, 0), (0, MLA_HEAD_STRIDE - MLA_NOPE - MLA_ROPE)))
    uq = jnp.concatenate([uq[:, :, :MLA_NOPE], _rope_cols(uq[:, :, MLA_NOPE:])], axis=2)
    uq = uq.reshape(MLA_Q_RANK, MLA_HEADS * MLA_HEAD_STRIDE).astype(BF16)
    ukv = w_ukv.reshape(MLA_KV_RANK, MLA_HEADS, MLA_NOPE + MLA_V)
    ukv = jnp.concatenate([ukv[:, :, :MLA_NOPE].reshape(MLA_KV_RANK, -1),
                           ukv[:, :, MLA_NOPE:].reshape(MLA_KV_RANK, -1)], axis=1).astype(BF16)
    wgate = jnp.zeros((LANES, 2 * W_GLA_QK), F32)
    wgate = wgate.at[:GLA_GATE_RANK, :W_GLA_QK].set(w_gf)
    wgate = wgate.at[GLA_GATE_RANK:2 * GLA_GATE_RANK, W_GLA_QK:].set(w_gb)
    bgate = jnp.concatenate([b_gf, b_gb])[None, :]
    return w_pad, w_z, uq, ukv, wgate.astype(BF16), bgate


def kernel(x, norm_g, w_in, mla_q_norm_g, mla_w_uq, mla_kv_norm_g, mla_w_ukv, nat_rpb,
           gla_w_gate_f, gla_b_gate_f, gla_w_gate_b, gla_b_gate_b, gla_norm_g,
           w_proj_a, w_proj_b, w_proj_c, w_proj_d, w_merge, b_merge, w_out, final_norm_g):
    b, s, d = x.shape
    depth = w_in.shape[0]
    tm = min(512, s)
    tq = min(1024, s)
    tabs = _rope_tables(s)
    x2 = x.reshape(b * s, d)
    w_in_b = w_in.astype(BF16)
    for l in range(depth):
        w_pad, w_z, uq, ukv, wgate, bgate = _prep_layer(
            w_in_b[l], mla_w_uq[l], mla_w_ukv[l], gla_w_gate_f[l], gla_b_gate_f[l],
            gla_w_gate_b[l], gla_b_gate_b[l])
        (q, kc, va, d0, d1, d2, nat_qkv, gq, gk, gv, la) = _in_proj(
            x2, s, tabs, norm_g[l][None, :], w_pad, mla_q_norm_g[l][None, :], uq,
            mla_kv_norm_g[l][None, :], ukv, wgate, bgate, tm)
        sh = lambda t: t.reshape(b, s, t.shape[-1])
        o_a = _mla(sh(q), sh(kc), sh(va), tq)
        obs, lses = [], []
        for dil_qkv, (window, _) in zip((d0, d1, d2), DIL_PATTERNS):
            o_g, lse_g = _dilated_group(dil_qkv, window)
            obs.append(o_g)
            lses.append(lse_g)
        o_c = _nat(sh(nat_qkv), nat_rpb[l])
        o_d = _gla(sh(gq), sh(gk), sh(gv), sh(la), gla_norm_g[l].reshape(1, W_D))
        w_p = jnp.concatenate([w_proj_a[l], w_proj_b[l], w_proj_c[l], w_proj_d[l]], axis=0)
        x2 = _merge(x2, s, norm_g[l][None, :], o_a.reshape(b * s, W_A), obs, lses,
                    o_c.reshape(b * s, W_C), o_d.reshape(b * s, W_D), w_z,
                    w_merge[l].astype(BF16), b_merge[l][None, :], w_p.astype(BF16),
                    w_out[l].astype(BF16), final_norm_g[None, :], l == depth - 1, tm)
    return x2.reshape(b, s, d)
```

```python
import functools

import numpy as np
import jax
import jax.numpy as jnp
from jax import lax
from jax.experimental import pallas as pl
from jax.experimental.pallas import tpu as pltpu

F32 = jnp.float32
BF16 = jnp.bfloat16

GRID_W = 64
ROPE_THETA = 10000.0
NORM_EPS = 1e-6
NEG_INF = -1e30

MLA_HEADS = 4
MLA_NOPE = 128
MLA_ROPE = 64
MLA_V = 128
MLA_Q_RANK = 256
MLA_KV_RANK = 128
DIL_PATTERNS = ((128, 1), (512, 4), (2048, 16))
DIL_HEADS = 4
DIL_DH = 64
NAT_HEADS = 8
NAT_DH = 64
NAT_KH_MAX = 8
NAT_KW = 16
GLA_HEADS = 4
GLA_DK = 64
GLA_DV = 128
GLA_GATE_RANK = 16
GLA_TAU = 16.0
GLA_CHUNK = 64

W_A = MLA_HEADS * MLA_V
W_B = DIL_HEADS * DIL_DH
W_C = NAT_HEADS * NAT_DH
W_D = GLA_HEADS * GLA_DV
N_BRANCH = 4
W_DIL_GROUP = 3 * W_B
W_GLA_QK = GLA_HEADS * GLA_DK

LOG2_E = 1.4426950408889634
LANES = 128
MLA_HEAD_STRIDE = 256
MLA_SUB_QUERIES = 512


def _largest_divisor(n, candidates):
    return next((c for c in candidates if n % c == 0), 1)
VMEM_LIMIT = 56 * 1024 * 1024

_IN_SIZES = (MLA_Q_RANK, MLA_KV_RANK, MLA_ROPE, 3 * W_DIL_GROUP, W_C, W_C, W_C,
             W_GLA_QK, W_GLA_QK, W_D, GLA_GATE_RANK, GLA_GATE_RANK, W_A, W_B, W_C, W_D)
_IN_OFF = tuple(int(v) for v in np.concatenate([[0], np.cumsum(_IN_SIZES)]))

_P_Q = 0
_P_KV = _P_Q + MLA_Q_RANK
_P_KR = _P_KV + MLA_KV_RANK
_P_DIL = _P_KR + LANES
_P_NAT = _P_DIL + 3 * W_DIL_GROUP
_P_GQ = _P_NAT + 3 * W_C
_P_GK = _P_GQ + W_GLA_QK
_P_GV = _P_GK + W_GLA_QK
_P_GG = _P_GV + W_D
_P_END = _P_GG + LANES
_PIECE_STARTS = (_P_Q, _P_DIL, _P_NAT, _P_GG)


def _dot(a, b):
    return jnp.dot(a, b, preferred_element_type=F32)


def _dot_nt(a, b):
    return lax.dot_general(a, b, (((1,), (1,)), ((), ())), preferred_element_type=F32)


def _rms(x, g):
    return x * lax.rsqrt(jnp.mean(x * x, axis=-1, keepdims=True) + NORM_EPS) * g


def _rope_cols(w):
    half = MLA_ROPE // 2
    t = w.reshape(w.shape[:-1] + (w.shape[-1] // LANES, 2, 2, half))
    return jnp.swapaxes(t, -3, -2).reshape(w.shape)


def _rope128(x, cos, sin_signed):
    return x * cos + pltpu.roll(x, LANES // 2, 1) * sin_signed


def _sigmoid(x):
    return 1.0 / (1.0 + jnp.exp(-x))


def _log_sigmoid(x):
    return jnp.minimum(x, 0.0) - jnp.log1p(jnp.exp(-jnp.abs(x)))


def _params(*sem):
    return pltpu.CompilerParams(dimension_semantics=sem, vmem_limit_bytes=VMEM_LIMIT)


def _const_spec(shape):
    nd = len(shape)
    return pl.BlockSpec(shape, lambda *_: (0,) * nd, pipeline_mode=pl.Buffered(1))


_PERM_DILATIONS = tuple(r for _, r in DIL_PATTERNS if r > 1)


def _perm_slot(r):
    return _PERM_DILATIONS.index(r)


PERM_BLOCK = 256
IN_PROJ_NORM_ROWS = 64
IN_PROJ_LAG = 5
IN_PROJ_ORDER = "EM012NG"
MERGE_GATE_COLS = 512
MERGE_LAG = 2


def _perm_tables(inverse):
    tabs = []
    for r in _PERM_DILATIONS:
        p = np.zeros((PERM_BLOCK, PERM_BLOCK), np.float32)
        tok = np.arange(PERM_BLOCK)
        p[(tok % r) * (PERM_BLOCK // r) + tok // r, tok] = 1.0
        tabs.append(p.T if inverse else p)
    return jnp.asarray(np.stack(tabs), BF16)


def _interleave(gens, lag):
    gens = list(gens)
    alive = [True] * len(gens)
    step = 0
    while any(alive):
        for i, gen in enumerate(gens):
            if alive[i] and step >= i * lag:
                try:
                    next(gen)
                except StopIteration:
                    alive[i] = False
        step += 1


def _in_proj_kernel(x_ref, cos_ref, sin_ref, g_ref, w0_ref, w1_ref, w2_ref, w3_ref, qg_ref, wuq_ref, kvg_ref,
                    wukv_ref, wgate_ref, bgate_ref, perm_ref,
                    q_ref, kc_ref, va_ref, d0_ref, d1_ref, d2_ref, nat_ref, gq_ref, gk_ref,
                    gv_ref, la_ref, y_ref):
    w_refs = (w0_ref, w1_ref, w2_ref, w3_ref)
    blocks = [_in_proj_block(blk, x_ref, cos_ref, sin_ref, g_ref, w_refs, qg_ref, wuq_ref,
                             kvg_ref, wukv_ref, wgate_ref, bgate_ref, perm_ref, q_ref, kc_ref,
                             va_ref, (d0_ref, d1_ref, d2_ref), nat_ref, gq_ref, gk_ref, gv_ref,
                             la_ref, y_ref)
              for blk in range(x_ref.shape[0] // PERM_BLOCK)]
    _interleave(blocks, lag=IN_PROJ_LAG)


def _in_proj_block(blk, x_ref, cos_ref, sin_ref, g_ref, w_refs, qg_ref, wuq_ref, kvg_ref,
                   wukv_ref, wgate_ref, bgate_ref, perm_ref, q_ref, kc_ref, va_ref, d_refs,
                   nat_ref, gq_ref, gk_ref, gv_ref, la_ref, y_ref):
    r0 = blk * PERM_BLOCK
    rows = slice(r0, r0 + PERM_BLOCK)
    chunks = []
    for c0 in range(r0, r0 + PERM_BLOCK, IN_PROJ_NORM_ROWS):
        chunks.append(_rms(x_ref[c0:c0 + IN_PROJ_NORM_ROWS, :], g_ref[...]).astype(BF16))
        yield
    hb = jnp.concatenate(chunks, axis=0)
    cos, sin = cos_ref[rows, :], sin_ref[rows, :]

    def mm(a, width):
        k = max(i for i, start in enumerate(_PIECE_STARTS) if start <= a)
        off = a - _PIECE_STARTS[k]
        return _dot(hb, w_refs[k][:, off:off + width])

    def rope(t):
        return _rope128(t, cos, sin)

    def sec_mla():
        qn = _rms(mm(_P_Q, MLA_Q_RANK), qg_ref[...]).astype(BF16)
        yield
        for h in range(MLA_HEADS):
            base = h * MLA_HEAD_STRIDE
            qh = _dot(qn, wuq_ref[:, base:base + MLA_HEAD_STRIDE])
            q_ref[rows, base:base + LANES] = qh[:, :LANES].astype(BF16)
            q_ref[rows, base + LANES:base + 2 * LANES] = rope(qh[:, LANES:]).astype(BF16)
            yield
        kv_kr = mm(_P_KV, MLA_KV_RANK + LANES)
        kvn = _rms(kv_kr[:, :MLA_KV_RANK], kvg_ref[...]).astype(BF16)
        kpe = rope(kv_kr[:, MLA_KV_RANK:]).astype(BF16)
        yield
        k_nope = _dot(kvn, wukv_ref[:, :W_A])
        for h in range(MLA_HEADS):
            base = h * MLA_HEAD_STRIDE
            kc_ref[rows, base:base + LANES] = k_nope[:, h * LANES:(h + 1) * LANES].astype(BF16)
            kc_ref[rows, base + LANES:base + 2 * LANES] = kpe
        yield
        va_ref[rows, :] = _dot(kvn, wukv_ref[:, W_A:2 * W_A]).astype(BF16)
        yield

    def sec_dil(g):
        d_ref, (_, r) = d_refs[g], DIL_PATTERNS[g]
        base = _P_DIL + g * W_DIL_GROUP
        qk = mm(base, 2 * W_B)
        tiles = [(rope(qk[:, c:c + LANES]) * (DIL_DH ** -0.5 if c < W_B else 1.0)).astype(BF16)
                 for c in range(0, 2 * W_B, LANES)]
        if r == 1:
            for t, c in zip(tiles, range(0, 2 * W_B, LANES)):
                d_ref[0, 0, rows, c:c + LANES] = t
            yield
            d_ref[0, 0, rows, 2 * W_B:] = mm(base + 2 * W_B, W_B).astype(BF16)
            yield
            return
        for t, c in zip(tiles, range(0, 2 * W_B, LANES)):
            y_ref[g, blk, :, c:c + LANES] = t
        yield
        y_ref[g, blk, :, 2 * W_B:] = mm(base + 2 * W_B, W_B).astype(BF16)
        per = PERM_BLOCK // r
        y = _dot(perm_ref[_perm_slot(r)], y_ref[g, blk]).astype(BF16)
        d_ref[0, :, blk * per:(blk + 1) * per, :] = y.reshape(r, per, W_DIL_GROUP)
        yield

    def sec_nat():
        nat_ref[rows, :W_C] = (mm(_P_NAT, W_C) * (NAT_DH ** -0.5)).astype(BF16)
        yield
        for c in range(W_C, 3 * W_C, W_C):
            nat_ref[rows, c:c + W_C] = mm(_P_NAT + c, W_C).astype(BF16)
            yield

    def sec_gla():
        gq_ref[rows, :] = mm(_P_GQ, W_GLA_QK) * (GLA_DK ** -0.5)
        yield
        gk_ref[rows, :] = mm(_P_GK, W_GLA_QK)
        yield
        gv_ref[rows, :] = mm(_P_GV, W_D).astype(BF16)
        yield

    def sec_gates():
        gates = _dot(mm(_P_GG, LANES).astype(BF16), wgate_ref[...]) + bgate_ref[...]
        for c in range(0, 2 * W_GLA_QK, LANES):
            yield
            la_ref[rows, c:c + LANES] = _log_sigmoid(gates[:, c:c + LANES]) * (1.0 / GLA_TAU)
        yield

    sections = {"M": sec_mla, "0": lambda: sec_dil(0), "1": lambda: sec_dil(1),
                "2": lambda: sec_dil(2), "N": sec_nat, "G": sec_gla, "E": sec_gates}
    for key in IN_PROJ_ORDER:
        yield from sections[key]()


def _in_proj(x2, seq, tabs, g, w, qg, wuq, kvg, wukv, wgate, bgate, tm):
    n, d = x2.shape
    nseq = seq // tm
    batch = n // seq
    row = lambda width: pl.BlockSpec((tm, width), lambda i: (i, 0))
    tab = pl.BlockSpec((tm, LANES), lambda i: (i % nseq, 0))
    sub = lambda r, width: pl.BlockSpec((1, r, tm // r, width),
                                        lambda i: (i // nseq, 0, i % nseq, 0))
    sub_shape = lambda r, width, dt: jax.ShapeDtypeStruct((batch, r, seq // r, width), dt)
    dil = [r for _, r in DIL_PATTERNS]
    outs = [(4 * MLA_HEAD_STRIDE, BF16), (4 * MLA_HEAD_STRIDE, BF16), (W_A, BF16)]
    outs2 = [(3 * W_C, BF16), (W_GLA_QK, F32), (W_GLA_QK, F32), (W_D, BF16), (2 * W_GLA_QK, F32)]
    assert tm % PERM_BLOCK == 0
    perm = _perm_tables(False)
    consts = (g, *w, qg, wuq, kvg, wukv, wgate, bgate, perm)
    return pl.pallas_call(
        _in_proj_kernel,
        grid=(n // tm,),
        in_specs=[row(d), tab, tab] + [_const_spec(a.shape) for a in consts],
        out_specs=([row(wd) for wd, _ in outs] + [sub(r, W_DIL_GROUP) for r in dil]
                   + [row(wd) for wd, _ in outs2]),
        out_shape=([jax.ShapeDtypeStruct((n, wd), dt) for wd, dt in outs]
                   + [sub_shape(r, W_DIL_GROUP, BF16) for r in dil]
                   + [jax.ShapeDtypeStruct((n, wd), dt) for wd, dt in outs2]),
        scratch_shapes=[pltpu.VMEM((len(dil), tm // PERM_BLOCK, PERM_BLOCK, W_DIL_GROUP), BF16)],
        compiler_params=_params("parallel"),
        name="in_proj",
    )(x2, *tabs, *consts)


def _mla_kernel(q_ref, k_ref, v_ref, o_ref, vx_ref):
    @pl.when(pl.program_id(2) == 0)
    def _():
        vx_ref[:, :MLA_V] = v_ref[0]
        vx_ref[:, MLA_V:] = jnp.ones((vx_ref.shape[0], MLA_V), BF16)

    c = (MLA_NOPE + MLA_ROPE) ** -0.5 * LOG2_E
    sub = MLA_SUB_QUERIES
    nsub = q_ref.shape[1] // sub
    scores = [_dot_nt(q_ref[0, i * sub:(i + 1) * sub, :], k_ref[0]) for i in range(nsub)]
    for i, s in enumerate(scores):
        m = jnp.max(s, axis=-1, keepdims=True)
        p = jnp.exp2((s - m) * c).astype(BF16)
        ov = _dot(p, vx_ref[...])
        o_ref[0, i * sub:(i + 1) * sub, :] = (ov[:, :MLA_V] / ov[:, MLA_V:]).astype(o_ref.dtype)


def _mla(q, kc, va, tq):
    b, s, _ = q.shape
    return pl.pallas_call(
        _mla_kernel,
        grid=(b, MLA_HEADS, s // tq),
        in_specs=[pl.BlockSpec((1, tq, MLA_HEAD_STRIDE), lambda i, h, j: (i, j, h)),
                  pl.BlockSpec((1, s, MLA_HEAD_STRIDE), lambda i, h, j: (i, 0, h)),
                  pl.BlockSpec((1, s, MLA_V), lambda i, h, j: (i, 0, h))],
        out_specs=pl.BlockSpec((1, tq, MLA_V), lambda i, h, j: (i, j, h)),
        out_shape=jax.ShapeDtypeStruct((b, s, W_A), BF16),
        scratch_shapes=[pltpu.VMEM((s, 2 * MLA_V), BF16)],
        compiler_params=_params("parallel", "parallel", "arbitrary"),
        name="mla_attention",
    )(q, kc, va)


def _stack_heads(x128, rope_order=False):
    lane = lax.broadcasted_iota(jnp.int32, x128.shape, 1)
    first = ((lane & 32) == 0) if rope_order else (lane < 64)
    zero = jnp.zeros_like(x128)
    return jnp.concatenate([jnp.where(first, x128, zero), jnp.where(first, zero, x128)], axis=0)


def _unstack_heads(y2):
    m = y2.shape[0] // 2
    lane = lax.broadcasted_iota(jnp.int32, (m, y2.shape[1]), 1)
    return jnp.where(lane < 64, y2[:m], y2[m:])


def _pair_scores(q128, k128, rope_order=False):
    return _dot_nt(_stack_heads(q128, rope_order), k128)


def _pair_softmax_pv(s, bias, v128, want_lse):
    s = s + bias
    mx = jnp.max(s, axis=-1, keepdims=True)
    p = jnp.exp(s - mx)
    den = jnp.sum(p, axis=-1, keepdims=True)
    o = _unstack_heads(_dot(p.astype(BF16), v128) / den)
    if not want_lse:
        return o, None
    lse = jnp.broadcast_to(mx + jnp.log(den), (s.shape[0], LANES))
    return o, _unstack_heads(lse)


def _dil_kernel(qkv_ref, bias_ref, o_ref, lse_ref, s_ref, *, length, qb, win, radius, group):
    nblk = length // qb

    def body(i, carry):
        jobs = []
        for u in range(group):
            flat = i * group + u
            sub = flat // nblk
            q0 = pl.multiple_of((flat % nblk) * qb, qb)
            ks = pl.multiple_of(jnp.clip(q0 - radius, 0, length - win), 64)
            for c in range(0, W_B, LANES):
                q128 = qkv_ref[sub, pl.ds(q0, qb), c:c + LANES]
                k128 = qkv_ref[sub, pl.ds(ks, win), W_B + c:W_B + c + LANES]
                s_ref[len(jobs)] = _pair_scores(q128, k128, rope_order=True)
                jobs.append((sub, q0, ks, c))
        for slot, (sub, q0, ks, c) in enumerate(jobs):
            v128 = qkv_ref[sub, pl.ds(ks, win), 2 * W_B + c:2 * W_B + c + LANES]
            o, lse = _pair_softmax_pv(s_ref[slot], bias_ref[(q0 - ks) // 64], v128, True)
            o_ref[sub, pl.ds(q0, qb), c:c + LANES] = o.astype(o_ref.dtype)
            lse_ref[sub, pl.ds(q0, qb), c:c + LANES] = lse
        return carry

    lax.fori_loop(0, qkv_ref.shape[0] * nblk // group, body, 0)


def _dil_bias_table(qb, win, radius):
    nvar = (win - qb) // 64 + 1
    i = (np.arange(2 * qb) % qb)[None, :, None]
    j = np.arange(win)[None, None, :]
    delta = (64 * np.arange(nvar))[:, None, None]
    return jnp.asarray(np.where(np.abs(i + delta - j) <= radius, 0.0, NEG_INF), F32)


def _dilated_group(qkv, window):
    b, r, length, _ = qkv.shape
    radius = window // (2 * r)
    qb = min(128, length)
    win = min(qb + 2 * radius, length)
    assert length % qb == 0 and radius % 64 == 0 and qb % 64 == 0
    table = _dil_bias_table(qb, win, radius)
    nsub = max(1, r // 4)
    group = _largest_divisor(nsub * (length // qb), (4, 2))
    sub = lambda width: pl.BlockSpec((None, nsub, length, width), lambda i, m: (i, m, 0, 0))
    return pl.pallas_call(
        functools.partial(_dil_kernel, length=length, qb=qb, win=win, radius=radius, group=group),
        grid=(b, r // nsub),
        in_specs=[sub(W_DIL_GROUP), _const_spec(table.shape)],
        out_specs=[sub(W_B), sub(W_B)],
        out_shape=[jax.ShapeDtypeStruct((b, r, length, W_B), BF16),
                   jax.ShapeDtypeStruct((b, r, length, W_B), F32)],
        scratch_shapes=[pltpu.VMEM((group * DIL_HEADS // 2, 2 * qb, win), F32)],
        compiler_params=_params("parallel", "parallel"),
        name=f"dilated_attention_r{r}",
    )(qkv, table)


def _nat_bias_table(rpb):
    qc = np.arange(GRID_W)[:, None]
    kc = np.arange(GRID_W)[None, :]
    win_start = np.clip(qc - NAT_KW // 2, 0, GRID_W - NAT_KW)
    allowed = (kc >= win_start) & (kc < win_start + NAT_KW)
    n_row = 2 * NAT_KH_MAX - 1
    side = GRID_W - NAT_KW
    g = jnp.pad(rpb, ((0, 0), (0, 0), (side, side + 1)))
    a = jnp.broadcast_to(g[:, :, None, :], (NAT_HEADS, n_row, GRID_W, 2 * GRID_W))
    a = a.reshape(NAT_HEADS, n_row, 2 * GRID_W * GRID_W)
    a = a[:, :, GRID_W - 1:GRID_W - 1 + GRID_W * (2 * GRID_W - 1)]
    toep = a.reshape(NAT_HEADS, n_row, GRID_W, 2 * GRID_W - 1)[..., :GRID_W]
    toep = jnp.where(allowed[None, None], toep, NEG_INF).astype(F32)
    return jnp.concatenate([toep[:, :-1], toep[:, 1:]], axis=-1)


def _nat_kernel(q_ref, k_ref, v_ref, bias_ref, o_ref, s_ref, *, rows, group):
    kh = NAT_KH_MAX

    def body(i, carry):
        jobs = []
        for u in range(group):
            r = i * group + u
            rs = jnp.clip(r - kh // 2, 0, rows - kh)
            q0 = pl.multiple_of(r * GRID_W, GRID_W)
            k0 = pl.multiple_of(rs * GRID_W, GRID_W)
            s_ref[u] = _pair_scores(q_ref[0, pl.ds(q0, GRID_W), :],
                                    k_ref[0, pl.ds(k0, kh * GRID_W), :])
            jobs.append((q0, k0, rs - r + NAT_KH_MAX - 1))
        for u, (q0, k0, var) in enumerate(jobs):
            v128 = v_ref[0, pl.ds(k0, kh * GRID_W), :]
            bias = jnp.concatenate(
                [jnp.concatenate([bias_ref[hh, var + j] for j in range(0, kh, 2)], axis=1)
                 for hh in range(2)], axis=0)
            o, _ = _pair_softmax_pv(s_ref[u], bias, v128, False)
            o_ref[0, pl.ds(q0, GRID_W), :] = o.astype(o_ref.dtype)
        return carry

    lax.fori_loop(0, rows // group, body, 0)


def _nat(qkv, rpb):
    b, s, _ = qkv.shape
    rows = s // GRID_W
    assert rows >= NAT_KH_MAX and NAT_KH_MAX % 2 == 0
    table = _nat_bias_table(rpb)
    npair = NAT_HEADS // 2
    col = lambda off: pl.BlockSpec((1, s, LANES), lambda i, p: (i, 0, off + p))
    group = _largest_divisor(rows, (64, 32, 16, 8, 4, 2))
    return pl.pallas_call(
        functools.partial(_nat_kernel, rows=rows, group=group),
        grid=(b, npair),
        in_specs=[col(0), col(npair), col(2 * npair),
                  pl.BlockSpec((2,) + table.shape[1:], lambda i, p: (p, 0, 0, 0))],
        out_specs=pl.BlockSpec((1, s, LANES), lambda i, p: (i, 0, p)),
        out_shape=jax.ShapeDtypeStruct((b, s, W_C), BF16),
        scratch_shapes=[pltpu.VMEM((group, 2 * GRID_W, NAT_KH_MAX * GRID_W), F32)],
        compiler_params=_params("parallel", "parallel"),
        name="neighbourhood_attention",
    )(qkv, qkv, qkv, table)


def _split3(x):
    hi = x.astype(BF16)
    r1 = x - hi.astype(F32)
    mid = r1.astype(BF16)
    lo = (r1 - mid.astype(F32)).astype(BF16)
    return hi, mid, lo


def _gla_kernel(q_ref, k_ref, v_ref, laf_ref, lab_ref, g_ref, o_ref, accf_ref, accb_ref, st_ref,
                *, seq):
    c = GLA_CHUNK
    nchunk = seq // c
    ri = lax.broadcasted_iota(jnp.int32, (c, c), 0)
    ci = lax.broadcasted_iota(jnp.int32, (c, c), 1)
    r2 = lax.broadcasted_iota(jnp.int32, (2 * c, 2 * c), 0)
    c2 = lax.broadcasted_iota(jnp.int32, (2 * c, 2 * c), 1)
    same_head = (r2 >= c) == (c2 >= c)
    zero_v = jnp.zeros((c, GLA_DV), BF16)
    st_ref[...] = jnp.zeros_like(st_ref)

    dirs = ((laf_ref, accf_ref, ci <= ri, same_head & (c2 <= r2), c - 1, c // 2 - 1),
            (lab_ref, accb_ref, ci >= ri, same_head & (c2 >= r2), 0, c // 2))
    group = _largest_divisor(nchunk, (32, 16, 8, 4, 2))

    def body(i, carry):
        jobs = []
        for d, (la_ref, acc_ref, tri, keep, total_row, mid_row) in enumerate(dirs):
            for u in range(group):
                n = i * group + u
                r0 = pl.multiple_of((nchunk - 1 - n if d else n) * c, c)
                hi, mid, lo = _split3(la_ref[0, pl.ds(r0, c), :])
                tri_b = tri.astype(BF16)
                bcum = _dot(tri_b, hi) + _dot(tri_b, mid) + _dot(tri_b, lo)
                jobs.append(dict(d=d, r0=r0, bcum=bcum, keep=keep, acc=acc_ref,
                                 b_tot=bcum[total_row:total_row + 1, :],
                                 b_mid=bcum[mid_row:mid_row + 1, :]))
        for j in jobs:
            q = q_ref[0, pl.ds(j["r0"], c), :]
            k = k_ref[0, pl.ds(j["r0"], c), :]
            v = v_ref[0, pl.ds(j["r0"], c), :]
            q_in = _stack_heads(q * jnp.exp(j["bcum"] - j["b_mid"])).astype(BF16)
            k_in = _stack_heads(k * jnp.exp(j["b_mid"] - j["bcum"])).astype(BF16)
            k_dec = (k * jnp.exp(j["b_tot"] - j["bcum"])).astype(BF16)
            j["q_st"] = _stack_heads(q * jnp.exp(j["bcum"])).astype(BF16)
            j["kv_t"] = _dot(v.astype(F32).T.astype(BF16), k_dec)
            j["att"] = _dot_nt(q_in, k_in)
            j["v2"] = jnp.concatenate([jnp.concatenate([v[:, :GLA_DV], zero_v], axis=1),
                                       jnp.concatenate([zero_v, v[:, GLA_DV:]], axis=1)], axis=0)
        states = [st_ref[0], st_ref[1]]
        for j in jobs:
            state = states[j["d"]]
            j["inter"] = _dot_nt(j["q_st"], state.astype(BF16))
            states[j["d"]] = state * jnp.exp(j["b_tot"]) + j["kv_t"]
        st_ref[0], st_ref[1] = states
        for j in jobs:
            att = jnp.where(j["keep"], j["att"], 0.0).astype(BF16)
            tot = _dot(att, j["v2"]) + j["inter"]
            j["acc"][pl.ds(j["r0"], c), :] = jnp.concatenate(
                [tot[:c, :GLA_DV], tot[c:, GLA_DV:]], axis=1)
        return carry

    lax.fori_loop(0, nchunk // group, body, 0)
    for hh in range(2):
        sl = slice(hh * GLA_DV, (hh + 1) * GLA_DV)
        o_ref[0, :, sl] = _rms(accf_ref[:, sl] + accb_ref[:, sl], g_ref[:, sl]).astype(o_ref.dtype)


def _gla(gq, gk, gv, la, gain):
    b, s, _ = gq.shape
    npair = GLA_HEADS // 2
    qk = pl.BlockSpec((1, s, LANES), lambda i, p: (i, 0, p))
    return pl.pallas_call(
        functools.partial(_gla_kernel, seq=s),
        grid=(b, npair),
        in_specs=[qk, qk,
                  pl.BlockSpec((1, s, 2 * GLA_DV), lambda i, p: (i, 0, p)),
                  pl.BlockSpec((1, s, LANES), lambda i, p: (i, 0, p)),
                  pl.BlockSpec((1, s, LANES), lambda i, p: (i, 0, npair + p)),
                  pl.BlockSpec((1, 2 * GLA_DV), lambda i, p: (0, p))],
        out_specs=pl.BlockSpec((1, s, 2 * GLA_DV), lambda i, p: (i, 0, p)),
        out_shape=jax.ShapeDtypeStruct((b, s, W_D), BF16),
        scratch_shapes=[pltpu.VMEM((s, 2 * GLA_DV), F32), pltpu.VMEM((s, 2 * GLA_DV), F32),
                        pltpu.VMEM((2, 2 * GLA_DV, LANES), F32)],
        compiler_params=_params("parallel", "parallel"),
        name="gated_linear_attention",
    )(gq, gk, gv, la, la, gain)


def _merge_kernel(x_ref, g_ref, oa_ref, ob0_ref, ob1_ref, ob2_ref, l0_ref, l1_ref, l2_ref,
                  oc_ref, od_ref, wz_ref, wm_ref, bm_ref, wp_ref, wo_ref, fg_ref, perm_ref, y_ref,
                  *, final):
    blocks = [_merge_block(blk, x_ref, g_ref, oa_ref, (ob0_ref, ob1_ref, ob2_ref),
                           (l0_ref, l1_ref, l2_ref), oc_ref, od_ref, wz_ref, wm_ref, bm_ref,
                           wp_ref, wo_ref, fg_ref, perm_ref, y_ref, final)
              for blk in range(x_ref.shape[0] // PERM_BLOCK)]
    _interleave(blocks, lag=MERGE_LAG)


def _merge_block(blk, x_ref, g_ref, oa_ref, ob_refs, l_refs, oc_ref, od_ref, wz_ref, wm_ref,
                 bm_ref, wp_ref, wo_ref, fg_ref, perm_ref, y_ref, final):
    rows = slice(blk * PERM_BLOCK, (blk + 1) * PERM_BLOCK)
    x = x_ref[rows, :]
    d = x.shape[1]
    hb = _rms(x, g_ref[...]).astype(BF16)
    yield

    obs, lses = [], []
    for (_, r), o_ref, l_ref in zip(DIL_PATTERNS, ob_refs, l_refs):
        if r == 1:
            obs.append(o_ref[0, 0, rows, :].astype(F32))
            lses.append(l_ref[0, 0, rows, :])
            continue
        pinv = perm_ref[_perm_slot(r)]
        per = PERM_BLOCK // r
        sub_rows = slice(blk * per, (blk + 1) * per)
        obs.append(_dot(pinv, o_ref[0, :, sub_rows, :].reshape(PERM_BLOCK, W_B)))
        parts = _split3(l_ref[0, :, sub_rows, :].reshape(PERM_BLOCK, W_B))
        lses.append(sum(_dot(pinv, part) for part in parts))
        yield

    mx = jnp.maximum(jnp.maximum(lses[0], lses[1]), lses[2])
    es = [jnp.exp(l - mx) for l in lses]
    tot = es[0] + es[1] + es[2]
    o_b = sum(e / tot * o for e, o in zip(es, obs))
    yield

    branches = ((oa_ref, W_A), (None, W_B), (oc_ref, W_C), (od_ref, W_D))
    mixed = [jnp.zeros((PERM_BLOCK, MERGE_GATE_COLS), F32) for _ in range(d // MERGE_GATE_COLS)]
    off = 0
    for i, (o_ref, width) in enumerate(branches):
        o = o_b if o_ref is None else o_ref[rows, :].astype(F32)
        z = _dot(hb, wz_ref[:, off:off + width])
        u = (o * (z * _sigmoid(z))).astype(BF16)
        yield
        proj = _dot(u, wp_ref[off:off + width, :])
        yield
        for n, c in enumerate(range(0, d, MERGE_GATE_COLS)):
            cols = slice(i * d + c, i * d + c + MERGE_GATE_COLS)
            gate = _sigmoid(_dot(hb, wm_ref[:, cols]) + bm_ref[:, cols])
            mixed[n] = mixed[n] + gate * proj[:, c:c + MERGE_GATE_COLS]
            yield
        off += width
    y = x + _dot(jnp.concatenate(mixed, axis=1).astype(BF16), wo_ref[...])
    if final:
        y = _rms(y, fg_ref[...])
    y_ref[rows, :] = y


def _merge(x2, seq, g, oa, obs, lses, oc, od, wz, wm, bm, wp, wo, fg, final, tm):
    n, d = x2.shape
    nseq = seq // tm
    row = lambda width: pl.BlockSpec((tm, width), lambda i: (i, 0))
    sub = lambda t: pl.BlockSpec((1, t.shape[1], tm // t.shape[1], W_B),
                                 lambda i: (i // nseq, 0, i % nseq, 0))
    consts = (wz, wm, bm, wp, wo, fg, _perm_tables(True))
    return pl.pallas_call(
        functools.partial(_merge_kernel, final=final),
        grid=(n // tm,),
        in_specs=[row(d), _const_spec(g.shape), row(W_A)] + [sub(t) for t in (*obs, *lses)]
                 + [row(W_C), row(W_D)] + [_const_spec(a.shape) for a in consts],
        out_specs=row(d),
        out_shape=jax.ShapeDtypeStruct((n, d), F32),
        compiler_params=_params("parallel"),
        name="gate_merge_out",
    )(x2, g, oa, *obs, *lses, oc, od, *consts)


def _rope_tables(seq):
    inv = jnp.power(ROPE_THETA, -jnp.arange(0, MLA_ROPE, 2, dtype=F32) / MLA_ROPE)
    ang = jnp.arange(seq, dtype=jnp.int32).astype(F32)[:, None] * inv[None, :]
    cos, sin = jnp.cos(ang), jnp.sin(ang)
    return jnp.concatenate([cos] * 4, axis=-1), jnp.concatenate([-sin, -sin, sin, sin], axis=-1)


def _prep_layer(w_in, w_uq, w_ukv, w_gf, b_gf, w_gb, b_gb):
    d = w_in.shape[0]
    o = _IN_OFF
    zpad = lambda width: jnp.zeros((d, width), w_in.dtype)
    dil = w_in[:, o[3]:o[4]].reshape(d, len(DIL_PATTERNS), 3, W_B)
    dil = jnp.concatenate([_rope_cols(dil[:, :, :2]), dil[:, :, 2:]], axis=2)
    w_pad = (
        jnp.concatenate([
            w_in[:, o[0]:o[2]],
            _rope_cols(jnp.concatenate([w_in[:, o[2]:o[3]], zpad(LANES - MLA_ROPE)], axis=1))],
            axis=1).astype(BF16),
        dil.reshape(d, 3 * W_DIL_GROUP).astype(BF16),
        w_in[:, o[4]:o[10]].astype(BF16),
        jnp.concatenate([w_in[:, o[10]:o[12]], zpad(LANES - 2 * GLA_GATE_RANK)],
                        axis=1).astype(BF16))
    w_z = w_in[:, o[12]:o[16]].astype(BF16)
    uq = w_uq.reshape(MLA_Q_RANK, MLA_HEADS, MLA_NOPE + MLA_ROPE)
    uq = jnp.pad(uq, ((0, 0), (0, 0), (0, MLA_HEAD_STRIDE - MLA_NOPE - MLA_ROPE)))
    uq = jnp.concatenate([uq[:, :, :MLA_NOPE], _rope_cols(uq[:, :, MLA_NOPE:])], axis=2)
    uq = uq.reshape(MLA_Q_RANK, MLA_HEADS * MLA_HEAD_STRIDE).astype(BF16)
    ukv = w_ukv.reshape(MLA_KV_RANK, MLA_HEADS, MLA_NOPE + MLA_V)
    ukv = jnp.concatenate([ukv[:, :, :MLA_NOPE].reshape(MLA_KV_RANK, -1),
                           ukv[:, :, MLA_NOPE:].reshape(MLA_KV_RANK, -1)], axis=1).astype(BF16)
    wgate = jnp.zeros((LANES, 2 * W_GLA_QK), F32)
    wgate = wgate.at[:GLA_GATE_RANK, :W_GLA_QK].set(w_gf)
    wgate = wgate.at[GLA_GATE_RANK:2 * GLA_GATE_RANK, W_GLA_QK:].set(w_gb)
    bgate = jnp.concatenate([b_gf, b_gb])[None, :]
    return w_pad, w_z, uq, ukv, wgate.astype(BF16), bgate


def kernel(x, norm_g, w_in, mla_q_norm_g, mla_w_uq, mla_kv_norm_g, mla_w_ukv, nat_rpb,
           gla_w_gate_f, gla_b_gate_f, gla_w_gate_b, gla_b_gate_b, gla_norm_g,
           w_proj_a, w_proj_b, w_proj_c, w_proj_d, w_merge, b_merge, w_out, final_norm_g):
    b, s, d = x.shape
    depth = w_in.shape[0]
    tm = min(512, s)
    tq = min(1024, s)
    tabs = _rope_tables(s)
    x2 = x.reshape(b * s, d)
    w_in_b = w_in.astype(BF16)
    for l in range(depth):
        w_pad, w_z, uq, ukv, wgate, bgate = _prep_layer(
            w_in_b[l], mla_w_uq[l], mla_w_ukv[l], gla_w_gate_f[l], gla_b_gate_f[l],
            gla_w_gate_b[l], gla_b_gate_b[l])
        (q, kc, va, d0, d1, d2, nat_qkv, gq, gk, gv, la) = _in_proj(
            x2, s, tabs, norm_g[l][None, :], w_pad, mla_q_norm_g[l][None, :], uq,
            mla_kv_norm_g[l][None, :], ukv, wgate, bgate, tm)
        sh = lambda t: t.reshape(b, s, t.shape[-1])
        o_a = _mla(sh(q), sh(kc), sh(va), tq)
        obs, lses = [], []
        for dil_qkv, (window, _) in zip((d0, d1, d2), DIL_PATTERNS):
            o_g, lse_g = _dilated_group(dil_qkv, window)
            obs.append(o_g)
            lses.append(lse_g)
        o_c = _nat(sh(nat_qkv), nat_rpb[l])
        o_d = _gla(sh(gq), sh(gk), sh(gv), sh(la), gla_norm_g[l].reshape(1, W_D))
        w_p = jnp.concatenate([w_proj_a[l], w_proj_b[l], w_proj_c[l], w_proj_d[l]], axis=0)
        x2 = _merge(x2, s, norm_g[l][None, :], o_a.reshape(b * s, W_A), obs, lses,
                    o_c.reshape(b * s, W_C), o_d.reshape(b * s, W_D), w_z,
                    w_merge[l].astype(BF16), b_merge[l][None, :], w_p.astype(BF16),
                    w_out[l].astype(BF16), final_norm_g[None, :], l == depth - 1, tm)
    return x2.reshape(b, s, d)
```
